```python
import math
import jax
import jax.numpy as jnp
from jax import lax
import numpy as np

D_MODEL = 2048
BATCH = 4
SEQ = 2048
DEPTH = 2
DEC_BATCH = 128
DEC_SEQ = 8
PAST_LEN = 16384
PAGE_SIZE = 128

N_META = 16
BLOCK = 128
EPS = 1e-6
NEG = -1e30
N_BRANCH = 4
BRANCH_DIM = D_MODEL // 2
CONV_DIM = BRANCH_DIM
CONV_WIDTH = 31
MLA_NOPE = 128
MLA_ROPE = 64
MLA_V = 128
MLA_HEADS = BRANCH_DIM // MLA_V
MLA_RANK = D_MODEL // 8
ROPE_THETA = 10000.0
MLA_SCALE = (MLA_NOPE + MLA_ROPE) ** -0.5
ML_DH = 256
ML_HEADS = BRANCH_DIM // ML_DH
ML_CHUNK = 64
DF_DQK = 64
DF_DV = 128
DF_HEADS = BRANCH_DIM // DF_DV
DF_SCALE = DF_DQK ** -0.5
D_FF = 4 * D_MODEL
SPLITS = (2 * CONV_DIM,
          MLA_HEADS * (MLA_NOPE + MLA_ROPE),
          MLA_RANK + MLA_ROPE,
          3 * ML_HEADS * ML_DH,
          ML_HEADS * ML_DH,
          2 * ML_HEADS,
          DF_HEADS * 2 * DF_DQK,
          2 * DF_DQK,
          DF_DV,
          N_BRANCH * D_MODEL)
N_IN = sum(SPLITS)

kernel_name = 'hybrid_conv_mla_mlstm_diffattn_step'


def rmsnorm(x, g):
    xf = x.astype(jnp.float32)
    y = xf * lax.rsqrt(jnp.mean(xf * xf, axis=-1, keepdims=True) + EPS)
    return (y * g.astype(jnp.float32)).astype(x.dtype)


def layernorm(x, g, b):
    xf = x.astype(jnp.float32)
    xc = xf - jnp.mean(xf, axis=-1, keepdims=True)
    y = xc * lax.rsqrt(jnp.mean(xc * xc, axis=-1, keepdims=True) + EPS)
    return (y * g.astype(jnp.float32) + b.astype(jnp.float32)).astype(x.dtype)


def rope(x, pos):
    half = MLA_ROPE // 2
    freq = ROPE_THETA ** (-jnp.arange(half, dtype=jnp.float32) / half)
    ang = pos.astype(jnp.float32)[:, None] * freq
    ang = ang.reshape((ang.shape[0],) + (1,) * (x.ndim - 3) + (half,))
    cos, sin = jnp.cos(ang), jnp.sin(ang)
    xf = x.astype(jnp.float32)
    x1, x2 = xf[..., :half], xf[..., half:]
    return jnp.concatenate([x1 * cos - x2 * sin, x1 * sin + x2 * cos], axis=-1).astype(x.dtype)


def split_proj(z):
    parts = []
    off = 0
    for n in SPLITS:
        parts.append(z[..., off:off + n])
        off += n
    return parts


def alibi_slopes():
    return 2.0 ** (-8.0 * jnp.arange(1, DF_HEADS + 1, dtype=jnp.float32) / DF_HEADS)


def glu(a_g):
    return a_g[..., :CONV_DIM] * jax.nn.sigmoid(a_g[..., CONV_DIM:])


def conformer_conv(u_ext, lp):
    w = lp['conv_w'][:, None, :].astype(u_ext.dtype)
    y = lax.conv_general_dilated(u_ext, w, (1,), 'VALID',
                                 dimension_numbers=('NWC', 'WIO', 'NWC'),
                                 feature_group_count=CONV_DIM)
    y = layernorm(y + lp['conv_b'].astype(y.dtype), lp['conv_ln_g'], lp['conv_ln_b'])
    return jax.nn.silu(y)


def mla_tokens(mla_q, mla_kv, pos, lp):
    B, T = mla_q.shape[:2]
    q = mla_q.reshape(B, T, MLA_HEADS, MLA_NOPE + MLA_ROPE)
    q_lat = jnp.einsum('bthn,rhn->bthr', q[..., :MLA_NOPE], lp['mla_w_uk'])
    q_r = rope(q[..., MLA_NOPE:], pos)
    c = rmsnorm(mla_kv[..., :MLA_RANK], lp['mla_kv_g'])
    k_r = rope(mla_kv[..., MLA_RANK:], pos)
    return q_lat, q_r, jnp.concatenate([c, k_r], axis=-1)


def mla_scores(q_lat, q_r, rows):
    s = (jnp.einsum('bthr,bsr->bhts', q_lat, rows[..., :MLA_RANK], preferred_element_type=jnp.float32)
         + jnp.einsum('bthp,bsp->bhts', q_r, rows[..., MLA_RANK:], preferred_element_type=jnp.float32))
    return (s * MLA_SCALE)[:, None]


def mla_finish(o, lp):
    out = jnp.einsum('bhtr,rhv->bthv', o[:, 0], lp['mla_w_uv'].astype(jnp.float32))
    return out.reshape(out.shape[0], out.shape[1], MLA_HEADS * MLA_V)


def diff_tokens(df_q, df_k):
    B, T = df_q.shape[:2]
    return df_q.reshape(B, T, DF_HEADS, 2, DF_DQK), df_k.reshape(B, T, 2, DF_DQK)


def diff_scores(q, k, q_pos, k_pos):
    s = jnp.einsum('bthmd,bsmd->bmhts', q, k, preferred_element_type=jnp.float32) * DF_SCALE
    dist = jnp.abs(q_pos[:, None] - k_pos[None, :]).astype(jnp.float32)
    return s - alibi_slopes()[:, None, None] * dist


def diff_finish(o, lp):
    f32 = jnp.float32
    lam = (jnp.exp(jnp.sum(lp['diff_lq1'].astype(f32) * lp['diff_lk1'].astype(f32)))
           - jnp.exp(jnp.sum(lp['diff_lq2'].astype(f32) * lp['diff_lk2'].astype(f32)))
           + lp['lam_init'])
    d = o[:, 0] - lam * o[:, 1]
    d = rmsnorm(d, lp['diff_norm_g']) * (1.0 - lp['lam_init'])
    B, H, T, V = d.shape
    return jnp.swapaxes(d, 1, 2).reshape(B, T, H * V)


def masked_softmax_attend(s, mask, v):
    p = jax.nn.softmax(jnp.where(mask, s, NEG), axis=-1)
    return jnp.einsum('bmhts,bsv->bmhtv', p, v.astype(jnp.float32))


def prompt_attend(score_fn, q_arrays, values, key_valid):
    B, L = values.shape[:2]
    nb = L // BLOCK
    k_idx = jnp.arange(L)

    def one_block(args):
        bi, qs = args
        t_idx = bi * BLOCK + jnp.arange(BLOCK)
        mask = (k_idx[None, :] <= t_idx[:, None]) & key_valid[None, :]
        return masked_softmax_attend(score_fn(qs, t_idx), mask, values)

    qb = tuple(jnp.moveaxis(q.reshape((B, nb, BLOCK) + q.shape[2:]), 1, 0) for q in q_arrays)
    o = lax.map(one_block, (jnp.arange(nb), qb))
    o = jnp.moveaxis(o, 0, 3)
    return o.reshape(o.shape[:3] + (L, o.shape[-1]))


def paged_attend(page_fn, s_loc, v_loc, page_table):
    T = s_loc.shape[-1]
    s_loc = jnp.where(jnp.tril(jnp.ones((T, T), dtype=bool)), s_loc, NEG)
    m = jnp.max(s_loc, axis=-1)
    p = jnp.exp(s_loc - m[..., None])
    l = jnp.sum(p, axis=-1)
    acc = jnp.einsum('bmhts,bsv->bmhtv', p, v_loc.astype(jnp.float32))

    def step(carry, xs):
        m, l, acc = carry
        p_idx, phys = xs
        s, v = page_fn(phys, p_idx)
        m_new = jnp.maximum(m, jnp.max(s, axis=-1))
        alpha = jnp.exp(m - m_new)
        pr = jnp.exp(s - m_new[..., None])
        acc = acc * alpha[..., None] + jnp.einsum('bmhts,bsv->bmhtv', pr, v.astype(jnp.float32))
        return (m_new, l * alpha + jnp.sum(pr, axis=-1), acc), None

    n_pages = page_table.shape[1]
    (m, l, acc), _ = lax.scan(step, (m, l, acc), (jnp.arange(n_pages, dtype=jnp.int32), page_table.T))
    return acc / l[..., None]


def mlstm_tokens(ml_qkv, ml_if, lp):
    f32 = jnp.float32
    B, T = ml_qkv.shape[:2]
    qkv = ml_qkv.astype(f32).reshape(B, T, 3, ML_HEADS, ML_DH).transpose(2, 0, 3, 1, 4)
    q, k, v = qkv[0], qkv[1] * (ML_DH ** -0.5), qkv[2]
    g = ml_if.astype(f32).reshape(B, T, 2, ML_HEADS).transpose(2, 0, 3, 1)
    i = g[0] + lp['mlstm_b_i'].astype(f32)[None, :, None]
    lf = jax.nn.log_sigmoid(g[1] + lp['mlstm_b_f'].astype(f32)[None, :, None])
    return q, k, v, i, lf


def mlstm_chunk(state, xs):
    C, n, m = state
    q, k, v, i, lf = xs
    Lc = q.shape[2]
    b = jnp.cumsum(lf, axis=-1)
    D = b[..., :, None] - b[..., None, :] + i[..., None, :]
    D = jnp.where(jnp.tril(jnp.ones((Lc, Lc), dtype=bool)), D, NEG)
    g_prev = b + m[..., None]
    m_t = jnp.maximum(g_prev, jnp.max(D, axis=-1))
    a_prev = jnp.exp(g_prev - m_t)
    wqk = jnp.exp(D - m_t[..., None]) * jnp.einsum('bhtd,bhsd->bhts', q, k)
    num = a_prev[..., None] * jnp.einsum('bhvd,bhtd->bhtv', C, q) + jnp.einsum('bhts,bhsv->bhtv', wqk, v)
    den = a_prev * jnp.einsum('bhd,bhtd->bht', n, q) + jnp.sum(wqk, axis=-1)
    h = num / jnp.maximum(jnp.abs(den), jnp.exp(-m_t))[..., None]
    m_new = m_t[..., -1]
    w_end = jnp.exp(b[..., -1:] - b + i - m_new[..., None])
    a_end = jnp.exp(b[..., -1] + m - m_new)
    C_new = a_end[..., None, None] * C + jnp.einsum('bhsv,bhsd->bhvd', w_end[..., None] * v, k)
    n_new = a_end[..., None] * n + jnp.einsum('bhs,bhsd->bhd', w_end, k)
    return (C_new, n_new, m_new), h


def mlstm_scan(q, k, v, i, lf):
    B, NH, L, DH = q.shape
    nc = L // ML_CHUNK

    def chunks(a):
        return jnp.moveaxis(a.reshape(a.shape[:2] + (nc, ML_CHUNK) + a.shape[3:]), 2, 0)

    f32 = jnp.float32
    state0 = (jnp.zeros((B, NH, DH, DH), f32), jnp.zeros((B, NH, DH), f32), jnp.zeros((B, NH), f32))
    state, h = lax.scan(mlstm_chunk, state0, tuple(chunks(a) for a in (q, k, v, i, lf)))
    return jnp.moveaxis(h, 0, 2).reshape(B, NH, L, DH), state


def mlstm_finish(h, ml_o, lp):
    B, NH, T, DH = h.shape
    hn = rmsnorm(jnp.swapaxes(h, 1, 2), lp['mlstm_norm_g'])
    o = jax.nn.sigmoid(ml_o.astype(jnp.float32)).reshape(B, T, NH, DH)
    return (o * hn).reshape(B, T, NH * DH)


def prompt_mixers(lp, parts, pos, valid):
    conv_ag, mla_q, mla_kv, ml_qkv, ml_o, ml_if, df_q, df_k, df_v, _ = parts
    L = conv_ag.shape[1]
    pad = BLOCK - N_META
    u = glu(conv_ag)
    a_out = conformer_conv(jnp.pad(u, ((0, 0), (CONV_WIDTH - 1, 0), (0, 0))), lp)
    conv_state = u[:, L - (CONV_WIDTH - 1):]
    q_lat, q_r, mrows = mla_tokens(mla_q, mla_kv, pos, lp)
    o = prompt_attend(lambda qs, t: mla_scores(qs[0], qs[1], mrows), (q_lat, q_r),
                      mrows[..., :MLA_RANK], valid)
    b_out = mla_finish(o, lp)
    q, k, v, i, lf = mlstm_tokens(ml_qkv, ml_if, lp)
    i = jnp.where(valid, i, NEG)
    lf = jnp.where(valid, lf, 0.0)
    h, (C, n, m) = mlstm_scan(q, k, v, i, lf)
    c_out = mlstm_finish(h, ml_o, lp)
    qd, kd = diff_tokens(df_q, df_k)
    o = prompt_attend(lambda qs, t: diff_scores(qs[0], kd, t - pad, pos), (qd,), df_v, valid)
    d_out = diff_finish(o, lp)
    return [a_out, b_out, c_out, d_out], (mrows[:, pad:], df_k[:, pad:], df_v[:, pad:], conv_state, C, n, m)


def sample_mixers(lp, parts, cache_mla, cache_dk, cache_dv, state_conv, state_C, state_n, state_m, page_table):
    l = lp['layer']
    f32 = jnp.float32
    conv_ag, mla_q, mla_kv, ml_qkv, ml_o, ml_if, df_q, df_k, df_v, _ = parts
    B, T = conv_ag.shape[:2]
    pos = PAST_LEN + jnp.arange(T, dtype=jnp.int32)
    u_ext = jnp.concatenate([state_conv[l].astype(conv_ag.dtype), glu(conv_ag)], axis=1)
    a_out = conformer_conv(u_ext, lp)
    conv_state = u_ext[:, T:]
    q_lat, q_r, mrows = mla_tokens(mla_q, mla_kv, pos, lp)

    def mla_page(phys, p_idx):
        rows = cache_mla[l, phys]
        return mla_scores(q_lat, q_r, rows), rows[..., :MLA_RANK]

    o = paged_attend(mla_page, mla_scores(q_lat, q_r, mrows), mrows[..., :MLA_RANK], page_table)
    b_out = mla_finish(o, lp)
    q, k, v, i, lf = mlstm_tokens(ml_qkv, ml_if, lp)
    (C, n, m), h = mlstm_chunk((state_C[l].astype(f32), state_n[l].astype(f32), state_m[l].astype(f32)),
                               (q, k, v, i, lf))
    c_out = mlstm_finish(h, ml_o, lp)
    qd, kd = diff_tokens(df_q, df_k)

    def diff_page(phys, p_idx):
        k_pos = p_idx * PAGE_SIZE + jnp.arange(PAGE_SIZE, dtype=jnp.int32)
        kp = cache_dk[l, phys].reshape(B, PAGE_SIZE, 2, DF_DQK)
        return diff_scores(qd, kp, pos, k_pos), cache_dv[l, phys]

    o = paged_attend(diff_page, diff_scores(qd, kd, pos, pos), df_v, page_table)
    d_out = diff_finish(o, lp)
    return [a_out, b_out, c_out, d_out], (mrows, df_k, df_v, conv_state, C, n, m)


def merge_and_mlp(h, branches, gate_pre, lp):
    B, T, _ = h.shape
    br = jnp.stack([b.astype(h.dtype) for b in branches], axis=2)
    proj = jnp.einsum('btkc,kcd->btkd', br, lp['w_branch'])
    gates = jax.nn.sigmoid(gate_pre.reshape(B, T, N_BRANCH, D_MODEL))
    h = h + jnp.einsum('btd,de->bte', jnp.sum(gates * proj, axis=2), lp['w_out'])
    hid = jax.nn.relu(jnp.einsum('btd,df->btf', rmsnorm(h, lp['norm_mlp_g']), lp['mlp_w1']))
    return h + jnp.einsum('btf,fd->btd', hid * hid, lp['mlp_w2'])


def trunk(h, lps, mixers, row_mask):
    states = []
    for l in range(DEPTH):
        lp = lps[l]
        xn = rmsnorm(h, lp['norm_mix_g'])
        parts = split_proj(jnp.einsum('btd,dn->btn', xn, lp['w_in']))
        branches, st = mixers(lp, parts)
        h = merge_and_mlp(h, branches, parts[-1], lp)
        if row_mask is not None:
            h = h * row_mask
        states.append(st)
    return h, states


def stack_states(states, j):
    return jnp.stack([s[j] for s in states], axis=0)


def setup_inputs(seed: int = 0) -> dict:
    key = jax.random.key(seed)
    keys = jax.random.split(key, 48)
    counter = [0]

    def nxt():
        k = keys[counter[0]]
        counter[0] += 1
        return k

    def nrm(shape, scale=1.0):
        a = jax.random.normal(nxt(), shape, jnp.float32)
        return a if scale == 1.0 else a * scale

    def gain(shape):
        return 1.0 + nrm(shape, 0.02)

    n_pages = PAST_LEN // PAGE_SIZE
    n_used = DEC_BATCH * n_pages
    n_pool = n_used + n_used // 4 + 1
    perm = jax.random.permutation(nxt(), n_pool)[:n_used]
    page_table = perm.reshape(DEC_BATCH, n_pages).astype(jnp.int32)
    d = D_MODEL
    return {
        'x_prompt': nrm((BATCH, SEQ, d)),
        'x_sample': nrm((DEC_BATCH, DEC_SEQ, d)),
        'cache_mla': nrm((DEPTH, n_pool, PAGE_SIZE, MLA_RANK + MLA_ROPE)),
        'cache_dk': nrm((DEPTH, n_pool, PAGE_SIZE, 2 * DF_DQK)),
        'cache_dv': nrm((DEPTH, n_pool, PAGE_SIZE, DF_DV)),
        'state_conv': nrm((DEPTH, DEC_BATCH, CONV_WIDTH - 1, CONV_DIM), 0.5),
        'state_mlstm_C': nrm((DEPTH, DEC_BATCH, ML_HEADS, ML_DH, ML_DH), 0.05),
        'state_mlstm_n': nrm((DEPTH, DEC_BATCH, ML_HEADS, ML_DH), 0.5),
        'state_mlstm_m': nrm((DEPTH, DEC_BATCH, ML_HEADS)),
        'page_table': page_table,
        'meta_tokens': nrm((N_META, d)),
        'norm_mix_g': gain((DEPTH, d)),
        'norm_mlp_g': gain((DEPTH, d)),
        'norm_final_g': gain((d,)),
        'w_in': nrm((DEPTH, d, N_IN), d ** -0.5),
        'conv_w': nrm((DEPTH, CONV_WIDTH, CONV_DIM), CONV_WIDTH ** -0.5),
        'conv_b': nrm((DEPTH, CONV_DIM), 0.01),
        'conv_ln_g': gain((DEPTH, CONV_DIM)),
        'conv_ln_b': nrm((DEPTH, CONV_DIM), 0.01),
        'mla_kv_g': gain((DEPTH, MLA_RANK)),
        'mla_w_uk': nrm((DEPTH, MLA_RANK, MLA_HEADS, MLA_NOPE), MLA_RANK ** -0.5),
        'mla_w_uv': nrm((DEPTH, MLA_RANK, MLA_HEADS, MLA_V), MLA_RANK ** -0.5),
        'mlstm_b_i': nrm((DEPTH, ML_HEADS), 0.1),
        'mlstm_b_f': jnp.linspace(3.0, 6.0, ML_HEADS, dtype=jnp.float32)[None, :] + nrm((DEPTH, ML_HEADS), 0.1),
        'mlstm_norm_g': gain((DEPTH, ML_HEADS, ML_DH)),
        'diff_lq1': nrm((DEPTH, DF_DQK), 0.1),
        'diff_lk1': nrm((DEPTH, DF_DQK), 0.1),
        'diff_lq2': nrm((DEPTH, DF_DQK), 0.1),
        'diff_lk2': nrm((DEPTH, DF_DQK), 0.1),
        'diff_norm_g': gain((DEPTH, DF_DV)),
        'w_branch': nrm((DEPTH, N_BRANCH, BRANCH_DIM, d), BRANCH_DIM ** -0.5),
        'w_out': nrm((DEPTH, d, d), d ** -0.5),
        'mlp_w1': nrm((DEPTH, d, D_FF), d ** -0.5),
        'mlp_w2': nrm((DEPTH, D_FF, d), D_FF ** -0.5),
    }


def reference(x_prompt, x_sample, cache_mla, cache_dk, cache_dv, state_conv, state_mlstm_C, state_mlstm_n,
              state_mlstm_m, page_table, meta_tokens, norm_mix_g, norm_mlp_g, norm_final_g, w_in, conv_w, conv_b,
              conv_ln_g, conv_ln_b, mla_kv_g, mla_w_uk, mla_w_uv, mlstm_b_i, mlstm_b_f, mlstm_norm_g, diff_lq1,
              diff_lk1, diff_lq2, diff_lk2, diff_norm_g, w_branch, w_out, mlp_w1, mlp_w2):
    lps = []
    for l in range(DEPTH):
        lps.append(dict(layer=l, lam_init=0.8 - 0.6 * math.exp(-0.3 * l),
                        norm_mix_g=norm_mix_g[l], norm_mlp_g=norm_mlp_g[l], w_in=w_in[l],
                        conv_w=conv_w[l], conv_b=conv_b[l], conv_ln_g=conv_ln_g[l], conv_ln_b=conv_ln_b[l],
                        mla_kv_g=mla_kv_g[l], mla_w_uk=mla_w_uk[l], mla_w_uv=mla_w_uv[l],
                        mlstm_b_i=mlstm_b_i[l], mlstm_b_f=mlstm_b_f[l], mlstm_norm_g=mlstm_norm_g[l],
                        diff_lq1=diff_lq1[l], diff_lk1=diff_lk1[l], diff_lq2=diff_lq2[l], diff_lk2=diff_lk2[l],
                        diff_norm_g=diff_norm_g[l], w_branch=w_branch[l], w_out=w_out[l],
                        mlp_w1=mlp_w1[l], mlp_w2=mlp_w2[l]))

    dt = x_prompt.dtype
    Bp, S = x_prompt.shape[:2]
    pad = BLOCK - N_META
    x_full = jnp.concatenate([jnp.zeros((Bp, pad, D_MODEL), dt),
                              jnp.broadcast_to(meta_tokens.astype(dt), (Bp, N_META, D_MODEL)),
                              x_prompt], axis=1)
    pos = jnp.arange(S + BLOCK, dtype=jnp.int32) - pad
    valid = pos >= 0
    h_p, st_p = trunk(x_full, lps, lambda lp, parts: prompt_mixers(lp, parts, pos, valid),
                      valid.astype(dt)[None, :, None])
    y_prompt = rmsnorm(h_p, norm_final_g)[:, BLOCK:]

    h_s, st_s = trunk(x_sample, lps,
                      lambda lp, parts: sample_mixers(lp, parts, cache_mla, cache_dk, cache_dv, state_conv,
                                                      state_mlstm_C, state_mlstm_n, state_mlstm_m, page_table),
                      None)
    y_sample = rmsnorm(h_s, norm_final_g)

    return (y_prompt, y_sample,
            stack_states(st_p, 0), stack_states(st_s, 0),
            stack_states(st_p, 1), stack_states(st_s, 1),
            stack_states(st_p, 2), stack_states(st_s, 2),
            stack_states(st_p, 3), stack_states(st_s, 3),
            stack_states(st_p, 4), stack_states(st_s, 4),
            stack_states(st_p, 5), stack_states(st_s, 5),
            stack_states(st_p, 6), stack_states(st_s, 6))
```

```python
import functools
import math

import jax
import jax.numpy as jnp
from jax import lax
from jax.experimental import pallas as pl
from jax.experimental.pallas import tpu as pltpu

F32 = jnp.float32

N_META_BLOCK = 128
EPS = 1e-6
NEG = -1e30
ROPE_THETA = 10000.0
ML_CHUNK = 64
N_BRANCH = 4

LANE = 128
SUBLANE = 8
VMEM_PHYSICAL_BYTES = 64 * 1024 * 1024
CONV_HIST_ROWS = 32
PAGES_PER_STEP = 16


def _cparams(sem, vmem_mib):
    return pltpu.CompilerParams(dimension_semantics=sem,
                                vmem_limit_bytes=min(int(vmem_mib * 1024 * 1024), VMEM_PHYSICAL_BYTES - (4 << 20)))


def _divisor(n, pref, mult):
    if n <= pref:
        return n
    d = (pref // mult) * mult
    while d >= mult:
        if n % d == 0:
            return d
        d -= mult
    raise ValueError(f"no block for {n} (pref {pref}, multiple of {mult})")


def _rms_rows(x, g):
    ms = jnp.mean(x * x, axis=-1, keepdims=True)
    return x * lax.rsqrt(ms + EPS) * g


def _norm_into(x_ref, g_ref, xn_ref, chunk):
    g = g_ref[...]

    def body(c, carry):
        r = pl.multiple_of(c * chunk, chunk)
        xn_ref[pl.ds(r, chunk), :] = _rms_rows(x_ref[pl.ds(r, chunk), :], g)
        return carry

    lax.fori_loop(0, x_ref.shape[0] // chunk, body, 0)


def _inproj_body(x_ref, g_ref, w_ref, o_ref, xn_ref, *, chunk):
    @pl.when(pl.program_id(1) == 0)
    def _():
        _norm_into(x_ref, g_ref, xn_ref, chunk)

    o_ref[...] = jnp.dot(xn_ref[...], w_ref[...], preferred_element_type=F32)


def _inproj(h, g, w, bm, bn):
    R, D = h.shape
    N = w.shape[1]
    chunk = _divisor(bm, 64, SUBLANE)
    vmem = (3 * bm * D + 2 * D * bn + 2 * bm * bn) * 4 / 2**20 + 6
    return pl.pallas_call(
        functools.partial(_inproj_body, chunk=chunk),
        out_shape=jax.ShapeDtypeStruct((R, N), F32),
        grid=(R // bm, N // bn),
        in_specs=[pl.BlockSpec((bm, D), lambda i, j: (i, 0)),
                  pl.BlockSpec((1, D), lambda i, j: (0, 0)),
                  pl.BlockSpec((D, bn), lambda i, j: (0, j))],
        out_specs=pl.BlockSpec((bm, bn), lambda i, j: (i, j)),
        scratch_shapes=[pltpu.VMEM((bm, D), F32)],
        compiler_params=_cparams(("parallel", "arbitrary"), vmem),
        name="inproj",
    )(h, g.reshape(1, D), w)


def _merge_body(*refs):
    gp = refs[0:N_BRANCH]
    br = refs[N_BRANCH:2 * N_BRANCH]
    wb_ref, o_ref = refs[2 * N_BRANCH], refs[2 * N_BRANCH + 1]
    acc = None
    for k in range(N_BRANCH):
        p = jnp.dot(br[k][...], wb_ref[k], preferred_element_type=F32)
        t = jax.nn.sigmoid(gp[k][...]) * p
        acc = t if acc is None else acc + t
    o_ref[...] = acc


def _merge(z, gate_off, branches, wb, bm, bn):
    R = z.shape[0]
    _, Cb, D = wb.shape
    gspecs = [pl.BlockSpec((bm, bn), functools.partial(lambda i, j, o: (i, o + j), o=(gate_off + k * D) // bn))
              for k in range(N_BRANCH)]
    bspecs = [pl.BlockSpec((bm, Cb), lambda i, j: (i, 0)) for _ in range(N_BRANCH)]
    vmem = (2 * N_BRANCH * bm * bn + 2 * N_BRANCH * bm * Cb + 2 * N_BRANCH * Cb * bn + 4 * bm * bn) * 4 / 2**20 + 6
    return pl.pallas_call(
        _merge_body,
        out_shape=jax.ShapeDtypeStruct((R, D), F32),
        grid=(R // bm, D // bn),
        in_specs=gspecs + bspecs + [pl.BlockSpec((N_BRANCH, Cb, bn), lambda i, j: (0, 0, j))],
        out_specs=pl.BlockSpec((bm, bn), lambda i, j: (i, j)),
        compiler_params=_cparams(("parallel", "arbitrary"), vmem),
        name="merge",
    )(*([z] * N_BRANCH), *branches, wb)


def _mm_res_body(x_ref, w_ref, r_ref, o_ref):
    o_ref[...] = r_ref[...] + jnp.dot(x_ref[...], w_ref[...], preferred_element_type=F32)


def _mm_res(x, w, res, bm, bn):
    R, K = x.shape
    N = w.shape[1]
    vmem = (2 * bm * K + 2 * K * bn + 4 * bm * bn) * 4 / 2**20 + 6
    return pl.pallas_call(
        _mm_res_body,
        out_shape=jax.ShapeDtypeStruct((R, N), F32),
        grid=(R // bm, N // bn),
        in_specs=[pl.BlockSpec((bm, K), lambda i, j: (i, 0)),
                  pl.BlockSpec((K, bn), lambda i, j: (0, j)),
                  pl.BlockSpec((bm, bn), lambda i, j: (i, j))],
        out_specs=pl.BlockSpec((bm, bn), lambda i, j: (i, j)),
        compiler_params=_cparams(("parallel", "arbitrary"), vmem),
        name="outproj",
    )(x, w, res)


def _mlp_body(x_ref, g_ref, w1_ref, w2_ref, mk_ref, o_ref, xn_ref, acc_ref, *, chunk):
    f = pl.program_id(1)

    @pl.when(f == 0)
    def _():
        _norm_into(x_ref, g_ref, xn_ref, chunk)
        acc_ref[...] = jnp.zeros_like(acc_ref)

    hid = jnp.maximum(jnp.dot(xn_ref[...], w1_ref[...], preferred_element_type=F32), 0.0)
    acc_ref[...] += jnp.dot(hid * hid, w2_ref[...], preferred_element_type=F32)

    @pl.when(f == pl.num_programs(1) - 1)
    def _():
        o_ref[...] = (x_ref[...] + acc_ref[...]) * mk_ref[...]


def _mlp(h, g, w1, w2, row_mask, bm, bf):
    R, D = h.shape
    FF = w1.shape[1]
    chunk = _divisor(bm, 64, SUBLANE)
    vmem = (6 * bm * D + 4 * D * bf + 2 * bm * bf + 2 * bm * LANE) * 4 / 2**20 + 6
    return pl.pallas_call(
        functools.partial(_mlp_body, chunk=chunk),
        out_shape=jax.ShapeDtypeStruct((R, D), F32),
        grid=(R // bm, FF // bf),
        in_specs=[pl.BlockSpec((bm, D), lambda i, f: (i, 0)),
                  pl.BlockSpec((1, D), lambda i, f: (0, 0)),
                  pl.BlockSpec((D, bf), lambda i, f: (0, f)),
                  pl.BlockSpec((bf, D), lambda i, f: (f, 0)),
                  pl.BlockSpec((bm, 1), lambda i, f: (i, 0))],
        out_specs=pl.BlockSpec((bm, D), lambda i, f: (i, 0)),
        scratch_shapes=[pltpu.VMEM((bm, D), F32), pltpu.VMEM((bm, D), F32)],
        compiler_params=_cparams(("parallel", "arbitrary"), vmem),
        name="mlp",
    )(h, g.reshape(1, D), w1, w2, row_mask)


def _final_norm_body(x_ref, g_ref, o_ref):
    o_ref[...] = _rms_rows(x_ref[...], g_ref[...])


def _final_norm(h, g, bm):
    R, D = h.shape
    return pl.pallas_call(
        _final_norm_body,
        out_shape=jax.ShapeDtypeStruct((R, D), F32),
        grid=(R // bm,),
        in_specs=[pl.BlockSpec((bm, D), lambda i: (i, 0)), pl.BlockSpec((1, D), lambda i: (0, 0))],
        out_specs=pl.BlockSpec((bm, D), lambda i: (i, 0)),
        compiler_params=_cparams(("parallel",), 16),
        name="final_norm",
    )(h, g.reshape(1, D))


def _conv_body(*refs, tb, C, CW, has_prev):
    if has_prev:
        ag_ref, prev_ref, hist_ref, cw_ref, cb_ref, lg_ref, lb_ref, a_ref, u_ref, ubuf, ybuf = refs
    else:
        ag_ref, hist_ref, cw_ref, cb_ref, lg_ref, lb_ref, a_ref, u_ref, ubuf, ybuf = refs
    HP = CONV_HIST_ROWS
    ag = ag_ref[...]
    u = ag[:, :C] * jax.nn.sigmoid(ag[:, C:])
    u_ref[...] = u
    ubuf[HP:, :] = u
    if has_prev:
        first = pl.program_id(1) == 0

        @pl.when(first)
        def _():
            ubuf[:HP, :] = hist_ref[0]

        @pl.when(jnp.logical_not(first))
        def _():
            p = prev_ref[...]
            ubuf[:HP, :] = p[:, :C] * jax.nn.sigmoid(p[:, C:])
    else:
        ubuf[:HP, :] = hist_ref[0]
    off = HP - (CW - 1)

    def cbody(cc, carry):
        c0 = pl.multiple_of(cc * LANE, LANE)
        acc = jnp.zeros((tb, LANE), F32)
        for w in range(CW):
            acc = acc + ubuf[pl.ds(off + w, tb), pl.ds(c0, LANE)] * cw_ref[pl.ds(w, 1), pl.ds(c0, LANE)]
        ybuf[:, pl.ds(c0, LANE)] = acc + cb_ref[:, pl.ds(c0, LANE)]
        return carry

    lax.fori_loop(0, C // LANE, cbody, 0)
    y = ybuf[...]
    yc = y - jnp.mean(y, axis=-1, keepdims=True)
    yn = yc * lax.rsqrt(jnp.mean(yc * yc, axis=-1, keepdims=True) + EPS) * lg_ref[...] + lb_ref[...]
    a_ref[...] = yn * jax.nn.sigmoid(yn)


def _conv(z, row0, nb, L, tb, hist, cw, cb, lg, lb, C):
    CW = cw.shape[0]
    HP = CONV_HIST_ROWS
    nt = L // tb
    has_prev = nt > 1
    cw_p = jnp.zeros((HP, C), F32).at[:CW].set(cw)
    rb0 = row0 // tb
    in_specs = [pl.BlockSpec((tb, 2 * C), lambda b, i: (rb0 + b * nt + i, 0))]
    args = [z]
    if has_prev:
        pb0 = row0 // HP
        in_specs.append(pl.BlockSpec((HP, 2 * C), lambda b, i: (jnp.maximum(pb0 + (b * L + i * tb) // HP - 1, 0), 0)))
        args.append(z)
    in_specs += [pl.BlockSpec((1, HP, C), lambda b, i: (b, 0, 0)),
                 pl.BlockSpec((HP, C), lambda b, i: (0, 0)),
                 pl.BlockSpec((1, C), lambda b, i: (0, 0)),
                 pl.BlockSpec((1, C), lambda b, i: (0, 0)),
                 pl.BlockSpec((1, C), lambda b, i: (0, 0))]
    args += [hist, cw_p, cb.reshape(1, C), lg.reshape(1, C), lb.reshape(1, C)]
    return pl.pallas_call(
        functools.partial(_conv_body, tb=tb, C=C, CW=CW, has_prev=has_prev),
        out_shape=(jax.ShapeDtypeStruct((nb * L, C), F32), jax.ShapeDtypeStruct((nb * L, C), F32)),
        grid=(nb, nt),
        in_specs=in_specs,
        out_specs=(pl.BlockSpec((tb, C), lambda b, i: (b * nt + i, 0)),
                   pl.BlockSpec((tb, C), lambda b, i: (b * nt + i, 0))),
        scratch_shapes=[pltpu.VMEM((HP + tb, C), F32), pltpu.VMEM((tb, C), F32)],
        compiler_params=_cparams(("parallel", "arbitrary"), 32),
        name="conv",
    )(*args)


def _mla_prep_body(zq_ref, cs_ref, sn_ref, wuk_ref, kvg_ref, qabs_ref, rows_ref, *, H, NOPE, ROPE, RANK, kv_off):
    cs = cs_ref[...]
    sn = sn_ref[...]
    half = ROPE // 2

    def rope(x):
        xs = jnp.concatenate([x[:, half:], x[:, :half]], axis=1)
        return x * cs + xs * sn

    for h in range(H):
        base = h * (NOPE + ROPE)
        qabs_ref[0, h, :, :RANK] = jnp.dot(zq_ref[:, base:base + NOPE], wuk_ref[h], preferred_element_type=F32)
        qabs_ref[0, h, :, RANK:] = rope(zq_ref[:, base + NOPE:base + NOPE + ROPE])
    rows_ref[:, :RANK] = _rms_rows(zq_ref[:, kv_off:kv_off + RANK], kvg_ref[...])
    rows_ref[:, RANK:] = rope(zq_ref[:, kv_off + RANK:kv_off + RANK + ROPE])


def _mla_prep(z, col_blk, wblk, tp, cs, sn, wuk_t, kvg, kv_off):
    R = z.shape[0]
    H, NOPE, RANK = wuk_t.shape
    ROPE = cs.shape[1]
    E = RANK + ROPE
    return pl.pallas_call(
        functools.partial(_mla_prep_body, H=H, NOPE=NOPE, ROPE=ROPE, RANK=RANK, kv_off=kv_off),
        out_shape=(jax.ShapeDtypeStruct((R // tp, H, tp, E), F32), jax.ShapeDtypeStruct((R, E), F32)),
        grid=(R // tp,),
        in_specs=[pl.BlockSpec((tp, wblk), lambda i: (i, col_blk)),
                  pl.BlockSpec((tp, ROPE), lambda i: (i, 0)),
                  pl.BlockSpec((tp, ROPE), lambda i: (i, 0)),
                  pl.BlockSpec((H, NOPE, RANK), lambda i: (0, 0, 0)),
                  pl.BlockSpec((1, RANK), lambda i: (0, 0))],
        out_specs=(pl.BlockSpec((1, H, tp, E), lambda i: (i, 0, 0, 0)),
                   pl.BlockSpec((tp, E), lambda i: (i, 0))),
        compiler_params=_cparams(("parallel",), 32),
        name="mla_prep",
    )(z, cs, sn, wuk_t, kvg.reshape(1, RANK))


def _softmax_step(s, v, m_ref, l_ref, acc_ref):
    m_prev = m_ref[...]
    m_new = jnp.maximum(m_prev, jnp.max(s, axis=1, keepdims=True))
    alpha = jnp.exp(m_prev - m_new)
    p = jnp.exp(s - m_new)
    l_ref[...] = alpha * l_ref[...] + jnp.sum(p, axis=1, keepdims=True)
    acc_ref[...] = alpha * acc_ref[...] + jnp.dot(p, v, preferred_element_type=F32)
    m_ref[...] = m_new


def _dot_t(a, b):
    return lax.dot_general(a, b, (((1,), (1,)), ((), ())), preferred_element_type=F32)


def _mla_prompt_body(q_ref, kv_ref, wuv_ref, o_ref, m_ref, l_ref, acc_ref, *, H, tq, RANK, V, scale, n_pad):
    qi = pl.program_id(1)
    tk = tq
    q = q_ref[0].reshape(H * tq, q_ref.shape[-1])
    m_ref[...] = jnp.full_like(m_ref, NEG)
    l_ref[...] = jnp.zeros_like(l_ref)
    acc_ref[...] = jnp.zeros_like(acc_ref)
    t_idx = qi * tq + lax.broadcasted_iota(jnp.int32, (H * tq, tk), 0) % tq
    lane = lax.broadcasted_iota(jnp.int32, (H * tq, tk), 1)

    def body(j, carry):
        k0 = pl.multiple_of(j * tk, tk)
        k = kv_ref[pl.ds(k0, tk), :]
        k_idx = k0 + lane
        s = jnp.where((k_idx <= t_idx) & (k_idx >= n_pad), _dot_t(q, k) * scale, NEG)
        _softmax_step(s, k[:, :RANK], m_ref, l_ref, acc_ref)
        return carry

    lax.fori_loop(0, qi + 1, body, 0)
    o = acc_ref[...] / l_ref[...]
    for h in range(H):
        o_ref[:, h * V:(h + 1) * V] = jnp.dot(o[h * tq:(h + 1) * tq], wuv_ref[h], preferred_element_type=F32)


def _mla_prompt(qabs, rows, wuv_t, B, L, tq, scale, n_pad):
    H, RANK, V = wuv_t.shape
    E = rows.shape[1]
    nq = L // tq
    return pl.pallas_call(
        functools.partial(_mla_prompt_body, H=H, tq=tq, RANK=RANK, V=V, scale=scale, n_pad=n_pad),
        out_shape=jax.ShapeDtypeStruct((B * L, H * V), F32),
        grid=(B, nq),
        in_specs=[pl.BlockSpec((1, H, tq, E), lambda b, i: (b * nq + i, 0, 0, 0)),
                  pl.BlockSpec((L, E), lambda b, i: (b, 0)),
                  pl.BlockSpec((H, RANK, V), lambda b, i: (0, 0, 0))],
        out_specs=pl.BlockSpec((tq, H * V), lambda b, i: (b * nq + i, 0)),
        scratch_shapes=[pltpu.VMEM((H * tq, 1), F32), pltpu.VMEM((H * tq, 1), F32), pltpu.VMEM((H * tq, RANK), F32)],
        compiler_params=_cparams(("parallel", "arbitrary"), 40),
        name="mla_prompt",
    )(qabs, rows, wuv_t)


def _mla_sample_body(pt_ref, q_ref, new_ref, *refs, H, T, RANK, V, scale, PP):
    pages = refs[:PP]
    wuv_ref, o_ref, m_ref, l_ref, acc_ref = refs[PP:]
    g = pl.program_id(1)
    q = q_ref[0, :, 0].reshape(H * T, q_ref.shape[-1])

    @pl.when(g == 0)
    def _():
        kn = new_ref[...]
        t_idx = lax.broadcasted_iota(jnp.int32, (H * T, T), 0) % T
        s_idx = lax.broadcasted_iota(jnp.int32, (H * T, T), 1)
        s = jnp.where(s_idx <= t_idx, _dot_t(q, kn) * scale, NEG)
        m0 = jnp.max(s, axis=1, keepdims=True)
        p = jnp.exp(s - m0)
        m_ref[...] = m0
        l_ref[...] = jnp.sum(p, axis=1, keepdims=True)
        acc_ref[...] = jnp.dot(p, kn[:, :RANK], preferred_element_type=F32)

    ss = [_dot_t(q, pg[0, 0]) * scale for pg in pages]
    mx = functools.reduce(jnp.maximum, [jnp.max(s, axis=1, keepdims=True) for s in ss])
    m_prev = m_ref[...]
    m_new = jnp.maximum(m_prev, mx)
    alpha = jnp.exp(m_prev - m_new)
    lsum = None
    pv = None
    for s, pg in zip(ss, pages):
        p = jnp.exp(s - m_new)
        ls = jnp.sum(p, axis=1, keepdims=True)
        d = jnp.dot(p, pg[0, 0][:, :RANK], preferred_element_type=F32)
        lsum = ls if lsum is None else lsum + ls
        pv = d if pv is None else pv + d
    l_ref[...] = alpha * l_ref[...] + lsum
    acc_ref[...] = alpha * acc_ref[...] + pv
    m_ref[...] = m_new

    @pl.when(g == pl.num_programs(1) - 1)
    def _():
        o = acc_ref[...] / l_ref[...]
        for h in range(H):
            o_ref[:, h * V:(h + 1) * V] = jnp.dot(o[h * T:(h + 1) * T], wuv_ref[h], preferred_element_type=F32)


def _mla_sample(qabs_s, rows, row0, cache, layer, pt_flat, wuv_t, Bs, T, n_pages, scale):
    H, RANK, V = wuv_t.shape
    E = rows.shape[1]
    per_blk = qabs_s.shape[2]
    PAGE = cache.shape[2]
    PP = min(PAGES_PER_STEP, n_pages)
    G = n_pages // PP
    rb0 = row0 // T
    page_specs = [pl.BlockSpec((1, 1, PAGE, E),
                               functools.partial(lambda b, g, pt, kk: (layer, pt[b * n_pages + g * PP + kk], 0, 0), kk=kk))
                  for kk in range(PP)]
    grid_spec = pltpu.PrefetchScalarGridSpec(
        num_scalar_prefetch=1,
        grid=(Bs, G),
        in_specs=[pl.BlockSpec((1, H, 1, T, E), lambda b, g, pt: (b // per_blk, 0, b % per_blk, 0, 0)),
                  pl.BlockSpec((T, E), lambda b, g, pt: (rb0 + b, 0))] + page_specs +
                 [pl.BlockSpec((H, RANK, V), lambda b, g, pt: (0, 0, 0))],
        out_specs=pl.BlockSpec((T, H * V), lambda b, g, pt: (b, 0)),
        scratch_shapes=[pltpu.VMEM((H * T, 1), F32), pltpu.VMEM((H * T, 1), F32), pltpu.VMEM((H * T, RANK), F32)],
    )
    return pl.pallas_call(
        functools.partial(_mla_sample_body, H=H, T=T, RANK=RANK, V=V, scale=scale, PP=PP),
        out_shape=jax.ShapeDtypeStruct((Bs * T, H * V), F32),
        grid_spec=grid_spec,
        compiler_params=_cparams(("parallel", "arbitrary"), 24),
        name="mla_sample",
    )(pt_flat, qabs_s, rows, *([cache] * PP), wuv_t)


def _diff_finish(a0, l0, a1, l1, lq1, lk1, lq2, lk2, ng, lam_init):
    lam = (jnp.exp(jnp.sum(lq1 * lk1, axis=1, keepdims=True))
           - jnp.exp(jnp.sum(lq2 * lk2, axis=1, keepdims=True)) + lam_init)
    d = a0 / l0 - lam * (a1 / l1)
    return _rms_rows(d, ng) * (1.0 - lam_init)


def _diff_prompt_body(q_ref, k_ref, v_ref, slope_ref, lq1_ref, lk1_ref, lq2_ref, lk2_ref, ng_ref, o_ref,
                      m0_ref, l0_ref, a0_ref, m1_ref, l1_ref, a1_ref, *, H, tq, DQK, DV, scale, n_pad, lam_init):
    qi = pl.program_id(1)
    tk = tq
    W = 2 * DQK
    q = jnp.concatenate([q_ref[:, h * W:(h + 1) * W] for h in range(H)], axis=0)
    for m_ref, l_ref, a_ref in ((m0_ref, l0_ref, a0_ref), (m1_ref, l1_ref, a1_ref)):
        m_ref[...] = jnp.full_like(m_ref, NEG)
        l_ref[...] = jnp.zeros_like(l_ref)
        a_ref[...] = jnp.zeros_like(a_ref)
    t_idx = qi * tq + lax.broadcasted_iota(jnp.int32, (H * tq, tk), 0) % tq
    lane = lax.broadcasted_iota(jnp.int32, (H * tq, tk), 1)
    first_map = lax.broadcasted_iota(jnp.int32, (tk, W), 1) < DQK
    slope = slope_ref[...]

    def body(j, carry):
        k0 = pl.multiple_of(j * tk, tk)
        k = k_ref[pl.ds(k0, tk), :]
        v = v_ref[pl.ds(k0, tk), :]
        k_idx = k0 + lane
        mask = (k_idx <= t_idx) & (k_idx >= n_pad)
        pen = slope * jnp.abs(t_idx - k_idx).astype(F32)
        s0 = jnp.where(mask, _dot_t(q, jnp.where(first_map, k, 0.0)) * scale - pen, NEG)
        _softmax_step(s0, v, m0_ref, l0_ref, a0_ref)
        s1 = jnp.where(mask, _dot_t(q, jnp.where(first_map, 0.0, k)) * scale - pen, NEG)
        _softmax_step(s1, v, m1_ref, l1_ref, a1_ref)
        return carry

    lax.fori_loop(0, qi + 1, body, 0)
    d = _diff_finish(a0_ref[...], l0_ref[...], a1_ref[...], l1_ref[...], lq1_ref[...], lk1_ref[...],
                     lq2_ref[...], lk2_ref[...], ng_ref[...], lam_init)
    for h in range(H):
        o_ref[:, h * DV:(h + 1) * DV] = d[h * tq:(h + 1) * tq]


def _diff_small_specs(DQK, DV, nargs):
    def cmap(*a):
        return (0, 0)
    return [pl.BlockSpec((1, DQK), cmap)] * 4 + [pl.BlockSpec((1, DV), cmap)]


def _diff_prompt(z, q_blk, k_blk, v_blk, slopes_col, lam_vecs, ng, B, L, tq, H, DQK, DV, scale, n_pad, lam_init):
    nq = L // tq
    W = 2 * DQK
    small = _diff_small_specs(DQK, DV, 2)
    sc = [pltpu.VMEM((H * tq, 1), F32), pltpu.VMEM((H * tq, 1), F32), pltpu.VMEM((H * tq, DV), F32)]
    return pl.pallas_call(
        functools.partial(_diff_prompt_body, H=H, tq=tq, DQK=DQK, DV=DV, scale=scale, n_pad=n_pad, lam_init=lam_init),
        out_shape=jax.ShapeDtypeStruct((B * L, H * DV), F32),
        grid=(B, nq),
        in_specs=[pl.BlockSpec((tq, H * W), lambda b, i: (b * nq + i, q_blk)),
                  pl.BlockSpec((L, W), lambda b, i: (b, k_blk)),
                  pl.BlockSpec((L, DV), lambda b, i: (b, v_blk)),
                  pl.BlockSpec((H * tq, 1), lambda b, i: (0, 0))] + small,
        out_specs=pl.BlockSpec((tq, H * DV), lambda b, i: (b * nq + i, 0)),
        scratch_shapes=sc + sc,
        compiler_params=_cparams(("parallel", "arbitrary"), 40),
        name="diff_prompt",
    )(z, z, z, slopes_col, *lam_vecs, ng.reshape(1, DV))


def _diff_sample_body(pt_ref, q_ref, kn_ref, vn_ref, *refs, H, T, DQK, DV, PAGE, past_len, scale, PP, lam_init):
    kpages = refs[:PP]
    vpages = refs[PP:2 * PP]
    (slope_ref, lq1_ref, lk1_ref, lq2_ref, lk2_ref, ng_ref, o_ref,
     m0_ref, l0_ref, a0_ref, m1_ref, l1_ref, a1_ref) = refs[2 * PP:]
    g = pl.program_id(1)
    W = 2 * DQK
    q = jnp.concatenate([q_ref[:, h * W:(h + 1) * W] for h in range(H)], axis=0)
    slope = slope_ref[...]
    states = ((m0_ref, l0_ref, a0_ref), (m1_ref, l1_ref, a1_ref))

    def split(k):
        fm = lax.broadcasted_iota(jnp.int32, k.shape, 1) < DQK
        return jnp.where(fm, k, 0.0), jnp.where(fm, 0.0, k)

    @pl.when(g == 0)
    def _():
        kn = kn_ref[...]
        vn = vn_ref[...]
        t_idx = lax.broadcasted_iota(jnp.int32, (H * T, T), 0) % T
        s_idx = lax.broadcasted_iota(jnp.int32, (H * T, T), 1)
        pen = slope * jnp.abs(t_idx - s_idx).astype(F32)
        for km, (m_ref, l_ref, a_ref) in zip(split(kn), states):
            s = jnp.where(s_idx <= t_idx, _dot_t(q, km) * scale - pen, NEG)
            m0 = jnp.max(s, axis=1, keepdims=True)
            p = jnp.exp(s - m0)
            m_ref[...] = m0
            l_ref[...] = jnp.sum(p, axis=1, keepdims=True)
            a_ref[...] = jnp.dot(p, vn, preferred_element_type=F32)

    q_pos = past_len + lax.broadcasted_iota(jnp.int32, (H * T, PAGE), 0) % T
    lane = lax.broadcasted_iota(jnp.int32, (H * T, PAGE), 1)
    pens = [slope * jnp.abs(q_pos - ((g * PP + kk) * PAGE + lane)).astype(F32) for kk in range(PP)]
    ksplit = [split(kp[0, 0]) for kp in kpages]
    for mi, (m_ref, l_ref, a_ref) in enumerate(states):
        ss = [_dot_t(q, ksplit[kk][mi]) * scale - pens[kk] for kk in range(PP)]
        mx = functools.reduce(jnp.maximum, [jnp.max(s, axis=1, keepdims=True) for s in ss])
        m_prev = m_ref[...]
        m_new = jnp.maximum(m_prev, mx)
        alpha = jnp.exp(m_prev - m_new)
        lsum = None
        pv = None
        for s, vp in zip(ss, vpages):
            p = jnp.exp(s - m_new)
            ls = jnp.sum(p, axis=1, keepdims=True)
            d = jnp.dot(p, vp[0, 0], preferred_element_type=F32)
            lsum = ls if lsum is None else lsum + ls
            pv = d if pv is None else pv + d
        l_ref[...] = alpha * l_ref[...] + lsum
        a_ref[...] = alpha * a_ref[...] + pv
        m_ref[...] = m_new

    @pl.when(g == pl.num_programs(1) - 1)
    def _():
        d = _diff_finish(a0_ref[...], l0_ref[...], a1_ref[...], l1_ref[...], lq1_ref[...], lk1_ref[...],
                         lq2_ref[...], lk2_ref[...], ng_ref[...], lam_init)
        for h in range(H):
            o_ref[:, h * DV:(h + 1) * DV] = d[h * T:(h + 1) * T]


def _diff_sample(z, row0, q_blk, k_blk, v_blk, cache_dk, cache_dv, layer, pt_flat, slopes_col, lam_vecs, ng,
                 Bs, T, H, DQK, DV, n_pages, scale, lam_init):
    W = 2 * DQK
    PAGE = cache_dk.shape[2]
    PP = min(PAGES_PER_STEP, n_pages)
    G = n_pages // PP
    rb0 = row0 // T

    def pmap(kk):
        return functools.partial(lambda b, g, pt, kk: (layer, pt[b * n_pages + g * PP + kk], 0, 0), kk=kk)

    kspecs = [pl.BlockSpec((1, 1, PAGE, W), pmap(kk)) for kk in range(PP)]
    vspecs = [pl.BlockSpec((1, 1, PAGE, DV), pmap(kk)) for kk in range(PP)]

    def cmap(b, g, pt):
        return (0, 0)

    small = [pl.BlockSpec((H * T, 1), cmap)] + [pl.BlockSpec((1, DQK), cmap)] * 4 + [pl.BlockSpec((1, DV), cmap)]
    sc = [pltpu.VMEM((H * T, 1), F32), pltpu.VMEM((H * T, 1), F32), pltpu.VMEM((H * T, DV), F32)]
    grid_spec = pltpu.PrefetchScalarGridSpec(
        num_scalar_prefetch=1,
        grid=(Bs, G),
        in_specs=[pl.BlockSpec((T, H * W), lambda b, g, pt: (rb0 + b, q_blk)),
                  pl.BlockSpec((T, W), lambda b, g, pt: (rb0 + b, k_blk)),
                  pl.BlockSpec((T, DV), lambda b, g, pt: (rb0 + b, v_blk))] + kspecs + vspecs + small,
        out_specs=pl.BlockSpec((T, H * DV), lambda b, g, pt: (b, 0)),
        scratch_shapes=sc + sc,
    )
    return pl.pallas_call(
        functools.partial(_diff_sample_body, H=H, T=T, DQK=DQK, DV=DV, PAGE=PAGE, past_len=n_pages * PAGE,
                          scale=scale, PP=PP, lam_init=lam_init),
        out_shape=jax.ShapeDtypeStruct((Bs * T, H * DV), F32),
        grid_spec=grid_spec,
        compiler_params=_cparams(("parallel", "arbitrary"), 24),
        name="diff_sample",
    )(pt_flat, z, z, z, *([cache_dk] * PP), *([cache_dv] * PP), slopes_col, *lam_vecs, ng.reshape(1, DV))


def _log_sigmoid(x):
    return jnp.minimum(x, 0.0) - jnp.log1p(jnp.exp(-jnp.abs(x)))


def _mlstm_body(q_ref, k_ref, v_ref, og_ref, gc_ref, gr_ref, bi_ref, bf_ref, ng_ref, c0_ref, n0_ref, m0_ref,
                h_ref, c_ref, n_ref, m_ref, *, NH, DH, Lc):
    @pl.when(pl.program_id(1) == 0)
    def _():
        c_ref[...] = c0_ref[...]
        n_ref[...] = n0_ref[...]
        m_ref[...] = m0_ref[...]

    row = lax.broadcasted_iota(jnp.int32, (Lc, Lc), 0)
    col = lax.broadcasted_iota(jnp.int32, (Lc, Lc), 1)
    tril = col <= row
    valid_c = gc_ref[:, 2 * NH:2 * NH + 1] > 0.0
    valid_r = gr_ref[0, 2 * NH:2 * NH + 1, :] > 0.0
    for hh in range(NH):
        sl = slice(hh * DH, (hh + 1) * DH)
        q = q_ref[:, sl]
        k = k_ref[:, sl] * (DH ** -0.5)
        v = v_ref[:, sl]
        b_i = bi_ref[:, hh:hh + 1]
        b_f = bf_ref[:, hh:hh + 1]
        i_c = jnp.where(valid_c, gc_ref[:, hh:hh + 1] + b_i, NEG)
        i_r = jnp.where(valid_r, gr_ref[0, hh:hh + 1, :] + b_i, NEG)
        lf_c = jnp.where(valid_c, _log_sigmoid(gc_ref[:, NH + hh:NH + hh + 1] + b_f), 0.0)
        lf_r = jnp.where(valid_r, _log_sigmoid(gr_ref[0, NH + hh:NH + hh + 1, :] + b_f), 0.0)
        b_c = jnp.sum(jnp.where(tril, lf_r, 0.0), axis=1, keepdims=True)
        b_r = jnp.sum(jnp.where(row <= col, lf_c, 0.0), axis=0, keepdims=True)
        C = c_ref[0, hh]
        n = n_ref[0, hh:hh + 1, :]
        m = m_ref[0, :, hh:hh + 1]
        D = jnp.where(tril, b_c - b_r + i_r, NEG)
        g_prev = b_c + m
        m_t = jnp.maximum(g_prev, jnp.max(D, axis=1, keepdims=True))
        a_prev = jnp.exp(g_prev - m_t)
        wqk = jnp.exp(D - m_t) * _dot_t(q, k)
        num = a_prev * _dot_t(q, C) + jnp.dot(wqk, v, preferred_element_type=F32)
        den = a_prev * jnp.sum(q * n, axis=1, keepdims=True) + jnp.sum(wqk, axis=1, keepdims=True)
        hv = num / jnp.maximum(jnp.abs(den), jnp.exp(-m_t))
        m_new = m_t[Lc - 1:Lc, :]
        b_last = b_c[Lc - 1:Lc, :]
        w_end = jnp.exp(b_last - b_c + i_c - m_new)
        a_end = jnp.exp(b_last + m - m_new)
        c_ref[0, hh] = a_end * C + lax.dot_general(w_end * v, k, (((0,), (0,)), ((), ())),
                                                   preferred_element_type=F32)
        n_ref[0, hh:hh + 1, :] = a_end * n + jnp.sum(w_end * k, axis=0, keepdims=True)
        m_ref[0, :, hh:hh + 1] = m_new
        hn = _rms_rows(hv, ng_ref[hh:hh + 1, :])
        h_ref[:, sl] = jax.nn.sigmoid(og_ref[:, sl]) * hn


def _mlstm(z, row0, nb, L, Lc, qkv_blk, gates_c, gates_r, b_i, b_f, ng, C0, n0, m0):
    NH, DH = ng.shape
    nc = L // Lc
    rb0 = row0 // Lc
    G = gates_c.shape[1]
    Wd = NH * DH

    def zspec(cb):
        return pl.BlockSpec((Lc, Wd), functools.partial(lambda b, c, cb: (rb0 + b * nc + c, cb), cb=cb))

    st_specs = [pl.BlockSpec((1, NH, DH, DH), lambda b, c: (b, 0, 0, 0)),
                pl.BlockSpec((1, NH, DH), lambda b, c: (b, 0, 0)),
                pl.BlockSpec((1, 1, NH), lambda b, c: (b, 0, 0))]
    return pl.pallas_call(
        functools.partial(_mlstm_body, NH=NH, DH=DH, Lc=Lc),
        out_shape=(jax.ShapeDtypeStruct((nb * L, Wd), F32),
                   jax.ShapeDtypeStruct((nb, NH, DH, DH), F32),
                   jax.ShapeDtypeStruct((nb, NH, DH), F32),
                   jax.ShapeDtypeStruct((nb, 1, NH), F32)),
        grid=(nb, nc),
        in_specs=[zspec(qkv_blk), zspec(qkv_blk + 1), zspec(qkv_blk + 2), zspec(qkv_blk + 3),
                  pl.BlockSpec((Lc, G), lambda b, c: (b * nc + c, 0)),
                  pl.BlockSpec((1, G, Lc), lambda b, c: (b * nc + c, 0, 0)),
                  pl.BlockSpec((1, NH), lambda b, c: (0, 0)),
                  pl.BlockSpec((1, NH), lambda b, c: (0, 0)),
                  pl.BlockSpec((NH, DH), lambda b, c: (0, 0))] + st_specs,
        out_specs=(pl.BlockSpec((Lc, Wd), lambda b, c: (b * nc + c, 0)), *st_specs),
        compiler_params=_cparams(("parallel", "arbitrary"), 32),
        name="mlstm",
    )(z, z, z, z, gates_c, gates_r, b_i.reshape(1, NH), b_f.reshape(1, NH), ng, C0, n0, m0.reshape(nb, 1, NH))


def _round_up(n, m):
    return (n + m - 1) // m * m


def kernel(x_prompt, x_sample, cache_mla, cache_dk, cache_dv, state_conv, state_mlstm_C, state_mlstm_n, state_mlstm_m, page_table, meta_tokens, norm_mix_g, norm_mlp_g, norm_final_g, w_in, conv_w, conv_b, conv_ln_g, conv_ln_b, mla_kv_g, mla_w_uk, mla_w_uv, mlstm_b_i, mlstm_b_f, mlstm_norm_g, diff_lq1, diff_lk1, diff_lq2, diff_lk2, diff_norm_g, w_branch, w_out, mlp_w1, mlp_w2):
    B, S, D = x_prompt.shape
    Bs, T, _ = x_sample.shape
    DEPTH = w_in.shape[0]
    n_meta = meta_tokens.shape[0]
    BLK = N_META_BLOCK
    pad = BLK - n_meta
    L = S + BLK
    Tp, Ts = B * L, Bs * T
    R = Tp + Ts
    C = conv_w.shape[2]
    CW = conv_w.shape[1]
    RANK, H, NOPE = mla_w_uk.shape[1:]
    V = mla_w_uv.shape[3]
    E = cache_mla.shape[3]
    ROPE = E - RANK
    NH, DH = mlstm_norm_g.shape[1:]
    DQK = diff_lq1.shape[1]
    DV = diff_norm_g.shape[1]
    Cb = w_branch.shape[2]
    HD = Cb // DV
    FF = mlp_w1.shape[2]
    n_pages = page_table.shape[1]
    PAGE = cache_mla.shape[2]
    past_len = n_pages * PAGE
    mla_scale = (NOPE + ROPE) ** -0.5
    df_scale = DQK ** -0.5

    GP = min(512, C)
    n_q, n_kv = H * (NOPE + ROPE), RANK + ROPE
    n_ml, n_dq, n_dk = NH * DH, HD * 2 * DQK, 2 * DQK
    off_a = 0
    off_b = off_a + 2 * C
    w_b = _round_up(n_q + n_kv, GP)
    off_c = off_b + w_b
    off_d = off_c + 4 * n_ml
    off_e = off_d + n_dq
    w_e = _round_up(n_dk + DV + 2 * NH, GP)
    off_g = off_e + w_e
    Nz = off_g + N_BRANCH * D
    assert off_b % w_b == 0 and off_c % n_ml == 0 and off_d % n_dq == 0 and off_e % n_dk == 0 and n_dk == DV
    splits = (2 * C, n_q, n_kv, 3 * n_ml, n_ml, 2 * NH, n_dq, n_dk, DV, N_BRANCH * D)
    src = [0]
    for n in splits:
        src.append(src[-1] + n)

    def pack_w(w):
        def cols(i):
            return w[:, src[i]:src[i + 1]]
        zb = jnp.zeros((D, w_b - n_q - n_kv), F32)
        ze = jnp.zeros((D, w_e - n_dk - DV - 2 * NH), F32)
        return jnp.concatenate([cols(0), cols(1), cols(2), zb, cols(3), cols(4), cols(6), cols(7), cols(8), cols(5),
                                ze, cols(9)], axis=1)

    off_if = off_e + n_dk + DV

    bm_big = _divisor(R, 1280, SUBLANE)
    bm_mid = _divisor(R, 640, SUBLANE)
    bn_in = _divisor(Nz, 512, GP)
    assert off_g % bn_in == 0
    bn_merge = _divisor(D, 256, LANE)
    bn_out = _divisor(D, 512, LANE)
    bf = _divisor(FF, 512, LANE)
    tq = BLK
    tp = _divisor(math.gcd(Tp, Ts), BLK, SUBLANE)
    assert tp == tq and tp % T == 0 and L % ML_CHUNK == 0 and L % tq == 0

    dt = x_prompt.dtype
    x_full = jnp.concatenate([jnp.zeros((B, pad, D), dt), jnp.broadcast_to(meta_tokens.astype(dt), (B, n_meta, D)),
                              x_prompt], axis=1)
    h = jnp.concatenate([x_full.reshape(Tp, D), x_sample.reshape(Ts, D)], axis=0)
    pos_p = jnp.arange(L, dtype=jnp.int32) - pad
    valid_p = (pos_p >= 0).astype(F32)
    pos_all = jnp.concatenate([jnp.tile(pos_p, B), jnp.tile(past_len + jnp.arange(T, dtype=jnp.int32), Bs)])
    row_mask = jnp.concatenate([jnp.tile(valid_p, B), jnp.ones((Ts,), F32)]).reshape(R, 1)
    half = ROPE // 2
    freq = ROPE_THETA ** (-jnp.arange(half, dtype=F32) / half)
    ang = pos_all.astype(F32)[:, None] * freq
    cos, sin = jnp.cos(ang), jnp.sin(ang)
    rope_cs = jnp.concatenate([cos, cos], axis=1)
    rope_sn = jnp.concatenate([-sin, sin], axis=1)
    slopes = 2.0 ** (-8.0 * jnp.arange(1, HD + 1, dtype=F32) / HD)
    slopes_p = jnp.repeat(slopes, tq).reshape(HD * tq, 1)
    slopes_s = jnp.repeat(slopes, T).reshape(HD * T, 1)
    pt_flat = page_table.reshape(-1).astype(jnp.int32)
    hist_p = jnp.zeros((B, CONV_HIST_ROWS, C), F32)
    zeros_C = jnp.zeros((B, NH, DH, DH), F32)
    zeros_n = jnp.zeros((B, NH, DH), F32)
    zeros_m = jnp.zeros((B, NH), F32)

    def gate_views(zif, valid, n_chunks, Lc):
        gc = jnp.concatenate([zif, valid[:, None]], axis=1)
        gr = gc.reshape(n_chunks, Lc, 2 * NH + 1).transpose(0, 2, 1)
        return gc, gr

    outs = [[] for _ in range(14)]
    for l in range(DEPTH):
        lam_init = 0.8 - 0.6 * math.exp(-0.3 * l)
        z = _inproj(h, norm_mix_g[l], pack_w(w_in[l]), bm_big, bn_in)

        a_p, u_p = _conv(z, 0, B, L, tq, hist_p, conv_w[l], conv_b[l], conv_ln_g[l], conv_ln_b[l], C)
        hist_s = jnp.pad(state_conv[l], ((0, 0), (CONV_HIST_ROWS - (CW - 1), 0), (0, 0)))
        a_s, u_s = _conv(z, Tp, Bs, T, T, hist_s, conv_w[l], conv_b[l], conv_ln_g[l], conv_ln_b[l], C)

        wuk_t = jnp.transpose(mla_w_uk[l], (1, 2, 0))
        wuv_t = jnp.transpose(mla_w_uv[l], (1, 0, 2))
        qabs, rows = _mla_prep(z, off_b // w_b, w_b, tp, rope_cs, rope_sn, wuk_t, mla_kv_g[l], n_q)
        b_p = _mla_prompt(qabs, rows, wuv_t, B, L, tq, mla_scale, pad)
        qabs_s = qabs[Tp // tp:].reshape(Ts // tp, H, tp // T, T, E)
        b_s = _mla_sample(qabs_s, rows, Tp, cache_mla, l, pt_flat, wuv_t, Bs, T, n_pages, mla_scale)

        zif = z[:, off_if:off_if + 2 * NH]
        gc_p, gr_p = gate_views(zif[:Tp], jnp.tile(valid_p, B), Tp // ML_CHUNK, ML_CHUNK)
        gc_s, gr_s = gate_views(zif[Tp:], jnp.ones((Ts,), F32), Bs, T)
        c_p, C_p, n_p, m_p = _mlstm(z, 0, B, L, ML_CHUNK, off_c // n_ml, gc_p, gr_p, mlstm_b_i[l], mlstm_b_f[l],
                                    mlstm_norm_g[l], zeros_C, zeros_n, zeros_m)
        c_s, C_s, n_s, m_s = _mlstm(z, Tp, Bs, T, T, off_c // n_ml, gc_s, gr_s, mlstm_b_i[l], mlstm_b_f[l],
                                    mlstm_norm_g[l], state_mlstm_C[l], state_mlstm_n[l], state_mlstm_m[l])

        lam_vecs = [v[l].reshape(1, DQK) for v in (diff_lq1, diff_lk1, diff_lq2, diff_lk2)]
        d_p = _diff_prompt(z, off_d // n_dq, off_e // n_dk, off_e // n_dk + 1, slopes_p, lam_vecs, diff_norm_g[l],
                           B, L, tq, HD, DQK, DV, df_scale, pad, lam_init)
        d_s = _diff_sample(z, Tp, off_d // n_dq, off_e // n_dk, off_e // n_dk + 1, cache_dk, cache_dv, l, pt_flat,
                           slopes_s, lam_vecs, diff_norm_g[l], Bs, T, HD, DQK, DV, n_pages, df_scale, lam_init)

        branches = [jnp.concatenate([p, s], axis=0) for p, s in ((a_p, a_s), (b_p, b_s), (c_p, c_s), (d_p, d_s))]
        mix = _merge(z, off_g, branches, w_branch[l], bm_mid, bn_merge)
        h = _mm_res(mix, w_out[l], h, bm_big, bn_out)
        h = _mlp(h, norm_mlp_g[l], mlp_w1[l], mlp_w2[l], row_mask, bm_mid, bf)

        dk_all = z[:, off_e:off_e + n_dk]
        dv_all = z[:, off_e + n_dk:off_e + n_dk + DV]
        layer_out = (
            rows[:Tp].reshape(B, L, E)[:, pad:], rows[Tp:].reshape(Bs, T, E),
            dk_all[:Tp].reshape(B, L, n_dk)[:, pad:], dk_all[Tp:].reshape(Bs, T, n_dk),
            dv_all[:Tp].reshape(B, L, DV)[:, pad:], dv_all[Tp:].reshape(Bs, T, DV),
            u_p.reshape(B, L, C)[:, L - (CW - 1):],
            jnp.concatenate([state_conv[l][:, T:], u_s.reshape(Bs, T, C)], axis=1),
            C_p, C_s, n_p, n_s, m_p.reshape(B, NH), m_s.reshape(Bs, NH))
        for o, v in zip(outs, layer_out):
            o.append(v)

    y = _final_norm(h, norm_final_g, _divisor(R, 512, SUBLANE))
    y_prompt = y[:Tp].reshape(B, L, D)[:, BLK:]
    y_sample = y[Tp:].reshape(Bs, T, D)
    return (y_prompt, y_sample) + tuple(jnp.stack(o, axis=0) for o in outs)
```

```python
import functools
import math

import jax
import jax.numpy as jnp
from jax import lax
from jax.experimental import pallas as pl
from jax.experimental.pallas import tpu as pltpu

F32 = jnp.float32

N_META_BLOCK = 128
EPS = 1e-6
NEG = -1e30
ROPE_THETA = 10000.0
ML_CHUNK = 64
N_BRANCH = 4

LANE = 128
SUBLANE = 8
VMEM_PHYSICAL_BYTES = 64 * 1024 * 1024
CONV_HIST_ROWS = 32
PAGES_PER_STEP = 32


def _cparams(sem, vmem_mib):
    return pltpu.CompilerParams(dimension_semantics=sem,
                                vmem_limit_bytes=min(int(vmem_mib * 1024 * 1024), VMEM_PHYSICAL_BYTES - (4 << 20)))


def _divisor(n, pref, mult):
    if n <= pref:
        return n
    d = (pref // mult) * mult
    while d >= mult:
        if n % d == 0:
            return d
        d -= mult
    raise ValueError(f"no block for {n} (pref {pref}, multiple of {mult})")


def _rms_rows(x, g):
    ms = jnp.mean(x * x, axis=-1, keepdims=True)
    return x * lax.rsqrt(ms + EPS) * g


def _norm_into(x_ref, g_ref, xn_ref, chunk):
    g = g_ref[...]

    def body(c, carry):
        r = pl.multiple_of(c * chunk, chunk)
        xn_ref[pl.ds(r, chunk), :] = _rms_rows(x_ref[pl.ds(r, chunk), :], g)
        return carry

    lax.fori_loop(0, x_ref.shape[0] // chunk, body, 0)


def _inproj_body(x_ref, g_ref, w_ref, o_ref, xn_ref, *, chunk):
    @pl.when(pl.program_id(1) == 0)
    def _():
        _norm_into(x_ref, g_ref, xn_ref, chunk)

    o_ref[...] = jnp.dot(xn_ref[...], w_ref[...], preferred_element_type=F32)


def _inproj(h, g, w, bm, bn):
    R, D = h.shape
    N = w.shape[1]
    chunk = _divisor(bm, 64, SUBLANE)
    vmem = (3 * bm * D + 2 * D * bn + 2 * bm * bn) * 4 / 2**20 + 6
    return pl.pallas_call(
        functools.partial(_inproj_body, chunk=chunk),
        out_shape=jax.ShapeDtypeStruct((R, N), F32),
        grid=(R // bm, N // bn),
        in_specs=[pl.BlockSpec((bm, D), lambda i, j: (i, 0)),
                  pl.BlockSpec((1, D), lambda i, j: (0, 0)),
                  pl.BlockSpec((D, bn), lambda i, j: (0, j))],
        out_specs=pl.BlockSpec((bm, bn), lambda i, j: (i, j)),
        scratch_shapes=[pltpu.VMEM((bm, D), F32)],
        compiler_params=_cparams(("parallel", "arbitrary"), vmem),
        name="inproj",
    )(h, g.reshape(1, D), w)


def _merge_body(*refs):
    gp = refs[0:N_BRANCH]
    br = refs[N_BRANCH:2 * N_BRANCH]
    wb_ref, o_ref = refs[2 * N_BRANCH], refs[2 * N_BRANCH + 1]
    acc = None
    for k in range(N_BRANCH):
        p = jnp.dot(br[k][...], wb_ref[k], preferred_element_type=F32)
        t = jax.nn.sigmoid(gp[k][...]) * p
        acc = t if acc is None else acc + t
    o_ref[...] = acc


def _merge(z, gate_off, branches, wb, bm, bn):
    R = z.shape[0]
    _, Cb, D = wb.shape
    gspecs = [pl.BlockSpec((bm, bn), functools.partial(lambda i, j, o: (i, o + j), o=(gate_off + k * D) // bn))
              for k in range(N_BRANCH)]
    bspecs = [pl.BlockSpec((bm, Cb), lambda i, j: (i, 0)) for _ in range(N_BRANCH)]
    vmem = (2 * N_BRANCH * bm * bn + 2 * N_BRANCH * bm * Cb + 2 * N_BRANCH * Cb * bn + 4 * bm * bn) * 4 / 2**20 + 6
    return pl.pallas_call(
        _merge_body,
        out_shape=jax.ShapeDtypeStruct((R, D), F32),
        grid=(R // bm, D // bn),
        in_specs=gspecs + bspecs + [pl.BlockSpec((N_BRANCH, Cb, bn), lambda i, j: (0, 0, j))],
        out_specs=pl.BlockSpec((bm, bn), lambda i, j: (i, j)),
        compiler_params=_cparams(("parallel", "arbitrary"), vmem),
        name="merge",
    )(*([z] * N_BRANCH), *branches, wb)


def _mm_res_body(x_ref, w_ref, r_ref, o_ref):
    o_ref[...] = r_ref[...] + jnp.dot(x_ref[...], w_ref[...], preferred_element_type=F32)


def _mm_res(x, w, res, bm, bn):
    R, K = x.shape
    N = w.shape[1]
    vmem = (2 * bm * K + 2 * K * bn + 4 * bm * bn) * 4 / 2**20 + 6
    return pl.pallas_call(
        _mm_res_body,
        out_shape=jax.ShapeDtypeStruct((R, N), F32),
        grid=(R // bm, N // bn),
        in_specs=[pl.BlockSpec((bm, K), lambda i, j: (i, 0)),
                  pl.BlockSpec((K, bn), lambda i, j: (0, j)),
                  pl.BlockSpec((bm, bn), lambda i, j: (i, j))],
        out_specs=pl.BlockSpec((bm, bn), lambda i, j: (i, j)),
        compiler_params=_cparams(("parallel", "arbitrary"), vmem),
        name="outproj",
    )(x, w, res)


def _mlp_body(x_ref, g_ref, w1_ref, w2_ref, mk_ref, o_ref, xn_ref, acc_ref, *, chunk):
    f = pl.program_id(1)

    @pl.when(f == 0)
    def _():
        _norm_into(x_ref, g_ref, xn_ref, chunk)
        acc_ref[...] = jnp.zeros_like(acc_ref)

    hid = jnp.maximum(jnp.dot(xn_ref[...], w1_ref[...], preferred_element_type=F32), 0.0)
    acc_ref[...] += jnp.dot(hid * hid, w2_ref[...], preferred_element_type=F32)

    @pl.when(f == pl.num_programs(1) - 1)
    def _():
        o_ref[...] = (x_ref[...] + acc_ref[...]) * mk_ref[...]


def _mlp(h, g, w1, w2, row_mask, bm, bf):
    R, D = h.shape
    FF = w1.shape[1]
    chunk = _divisor(bm, 64, SUBLANE)
    vmem = (6 * bm * D + 4 * D * bf + 2 * bm * bf + 2 * bm * LANE) * 4 / 2**20 + 6
    return pl.pallas_call(
        functools.partial(_mlp_body, chunk=chunk),
        out_shape=jax.ShapeDtypeStruct((R, D), F32),
        grid=(R // bm, FF // bf),
        in_specs=[pl.BlockSpec((bm, D), lambda i, f: (i, 0)),
                  pl.BlockSpec((1, D), lambda i, f: (0, 0)),
                  pl.BlockSpec((D, bf), lambda i, f: (0, f)),
                  pl.BlockSpec((bf, D), lambda i, f: (f, 0)),
                  pl.BlockSpec((bm, 1), lambda i, f: (i, 0))],
        out_specs=pl.BlockSpec((bm, D), lambda i, f: (i, 0)),
        scratch_shapes=[pltpu.VMEM((bm, D), F32), pltpu.VMEM((bm, D), F32)],
        compiler_params=_cparams(("parallel", "arbitrary"), vmem),
        name="mlp",
    )(h, g.reshape(1, D), w1, w2, row_mask)


def _final_norm_body(x_ref, g_ref, o_ref):
    o_ref[...] = _rms_rows(x_ref[...], g_ref[...])


def _final_norm(h, g, bm):
    R, D = h.shape
    return pl.pallas_call(
        _final_norm_body,
        out_shape=jax.ShapeDtypeStruct((R, D), F32),
        grid=(R // bm,),
        in_specs=[pl.BlockSpec((bm, D), lambda i: (i, 0)), pl.BlockSpec((1, D), lambda i: (0, 0))],
        out_specs=pl.BlockSpec((bm, D), lambda i: (i, 0)),
        compiler_params=_cparams(("parallel",), 16),
        name="final_norm",
    )(h, g.reshape(1, D))


def _conv_body(*refs, tb, C, CW, has_prev):
    if has_prev:
        ag_ref, prev_ref, hist_ref, cw_ref, cb_ref, lg_ref, lb_ref, a_ref, u_ref, ubuf, ybuf = refs
    else:
        ag_ref, hist_ref, cw_ref, cb_ref, lg_ref, lb_ref, a_ref, u_ref, ubuf, ybuf = refs
    HP = CONV_HIST_ROWS
    ag = ag_ref[...]
    u = ag[:, :C] * jax.nn.sigmoid(ag[:, C:])
    u_ref[...] = u
    ubuf[HP:, :] = u
    if has_prev:
        first = pl.program_id(1) == 0

        @pl.when(first)
        def _():
            ubuf[:HP, :] = hist_ref[0]

        @pl.when(jnp.logical_not(first))
        def _():
            p = prev_ref[...]
            ubuf[:HP, :] = p[:, :C] * jax.nn.sigmoid(p[:, C:])
    else:
        ubuf[:HP, :] = hist_ref[0]
    off = HP - (CW - 1)

    def cbody(cc, carry):
        c0 = pl.multiple_of(cc * LANE, LANE)
        acc = jnp.zeros((tb, LANE), F32)
        for w in range(CW):
            acc = acc + ubuf[pl.ds(off + w, tb), pl.ds(c0, LANE)] * cw_ref[pl.ds(w, 1), pl.ds(c0, LANE)]
        ybuf[:, pl.ds(c0, LANE)] = acc + cb_ref[:, pl.ds(c0, LANE)]
        return carry

    lax.fori_loop(0, C // LANE, cbody, 0)
    y = ybuf[...]
    yc = y - jnp.mean(y, axis=-1, keepdims=True)
    yn = yc * lax.rsqrt(jnp.mean(yc * yc, axis=-1, keepdims=True) + EPS) * lg_ref[...] + lb_ref[...]
    a_ref[...] = yn * jax.nn.sigmoid(yn)


def _conv(z, row0, nb, L, tb, hist, cw, cb, lg, lb, C):
    CW = cw.shape[0]
    HP = CONV_HIST_ROWS
    nt = L // tb
    has_prev = nt > 1
    cw_p = jnp.zeros((HP, C), F32).at[:CW].set(cw)
    rb0 = row0 // tb
    in_specs = [pl.BlockSpec((tb, 2 * C), lambda b, i: (rb0 + b * nt + i, 0))]
    args = [z]
    if has_prev:
        pb0 = row0 // HP
        in_specs.append(pl.BlockSpec((HP, 2 * C), lambda b, i: (jnp.maximum(pb0 + (b * L + i * tb) // HP - 1, 0), 0)))
        args.append(z)
    in_specs += [pl.BlockSpec((1, HP, C), lambda b, i: (b, 0, 0)),
                 pl.BlockSpec((HP, C), lambda b, i: (0, 0)),
                 pl.BlockSpec((1, C), lambda b, i: (0, 0)),
                 pl.BlockSpec((1, C), lambda b, i: (0, 0)),
                 pl.BlockSpec((1, C), lambda b, i: (0, 0))]
    args += [hist, cw_p, cb.reshape(1, C), lg.reshape(1, C), lb.reshape(1, C)]
    return pl.pallas_call(
        functools.partial(_conv_body, tb=tb, C=C, CW=CW, has_prev=has_prev),
        out_shape=(jax.ShapeDtypeStruct((nb * L, C), F32), jax.ShapeDtypeStruct((nb * L, C), F32)),
        grid=(nb, nt),
        in_specs=in_specs,
        out_specs=(pl.BlockSpec((tb, C), lambda b, i: (b * nt + i, 0)),
                   pl.BlockSpec((tb, C), lambda b, i: (b * nt + i, 0))),
        scratch_shapes=[pltpu.VMEM((HP + tb, C), F32), pltpu.VMEM((tb, C), F32)],
        compiler_params=_cparams(("parallel", "arbitrary"), 32),
        name="conv",
    )(*args)


def _mla_prep_body(zq_ref, cs_ref, sn_ref, wuk_ref, kvg_ref, qabs_ref, rows_ref, *, H, NOPE, ROPE, RANK, kv_off):
    cs = cs_ref[...]
    sn = sn_ref[...]
    half = ROPE // 2

    def rope(x):
        xs = jnp.concatenate([x[:, half:], x[:, :half]], axis=1)
        return x * cs + xs * sn

    for h in range(H):
        base = h * (NOPE + ROPE)
        qabs_ref[0, h, :, :RANK] = jnp.dot(zq_ref[:, base:base + NOPE], wuk_ref[h], preferred_element_type=F32)
        qabs_ref[0, h, :, RANK:] = rope(zq_ref[:, base + NOPE:base + NOPE + ROPE])
    rows_ref[:, :RANK] = _rms_rows(zq_ref[:, kv_off:kv_off + RANK], kvg_ref[...])
    rows_ref[:, RANK:] = rope(zq_ref[:, kv_off + RANK:kv_off + RANK + ROPE])


def _mla_prep(z, col_blk, wblk, tp, cs, sn, wuk_t, kvg, kv_off):
    R = z.shape[0]
    H, NOPE, RANK = wuk_t.shape
    ROPE = cs.shape[1]
    E = RANK + ROPE
    return pl.pallas_call(
        functools.partial(_mla_prep_body, H=H, NOPE=NOPE, ROPE=ROPE, RANK=RANK, kv_off=kv_off),
        out_shape=(jax.ShapeDtypeStruct((R // tp, H, tp, E), F32), jax.ShapeDtypeStruct((R, E), F32)),
        grid=(R // tp,),
        in_specs=[pl.BlockSpec((tp, wblk), lambda i: (i, col_blk)),
                  pl.BlockSpec((tp, ROPE), lambda i: (i, 0)),
                  pl.BlockSpec((tp, ROPE), lambda i: (i, 0)),
                  pl.BlockSpec((H, NOPE, RANK), lambda i: (0, 0, 0)),
                  pl.BlockSpec((1, RANK), lambda i: (0, 0))],
        out_specs=(pl.BlockSpec((1, H, tp, E), lambda i: (i, 0, 0, 0)),
                   pl.BlockSpec((tp, E), lambda i: (i, 0))),
        compiler_params=_cparams(("parallel",), 32),
        name="mla_prep",
    )(z, cs, sn, wuk_t, kvg.reshape(1, RANK))


def _dot_t(a, b):
    return lax.dot_general(a, b, (((1,), (1,)), ((), ())), preferred_element_type=F32)


def _softmax_step(s, v, m_ref, l_ref, acc_ref, trans_v=False):
    m_prev = m_ref[...]
    m_new = jnp.maximum(m_prev, jnp.max(s, axis=1, keepdims=True))
    alpha = jnp.exp(m_prev - m_new)
    p = jnp.exp(s - m_new)
    l_ref[...] = alpha * l_ref[...] + jnp.sum(p, axis=1, keepdims=True)
    pv = _dot_t(p, v) if trans_v else jnp.dot(p, v, preferred_element_type=F32)
    acc_ref[...] = alpha * acc_ref[...] + pv
    m_ref[...] = m_new


def _lanes(x, n):
    return x if n == 1 else jnp.concatenate([x] * n, axis=1)


def _softmax_step_wide(s, v, m_ref, acc_ref):
    w = s.shape[1]
    m_prev = m_ref[...]
    m_new = jnp.maximum(m_prev, jnp.max(s, axis=1, keepdims=True))
    alpha = jnp.exp(m_prev - m_new)
    p = jnp.exp(s - _lanes(m_new, w // LANE))
    v_ext = jnp.concatenate([v, jnp.ones((w, LANE), F32)], axis=1)
    acc_ref[...] = (_lanes(alpha, acc_ref.shape[1] // LANE) * acc_ref[...]
                    + jnp.dot(p, v_ext, preferred_element_type=F32))
    m_ref[...] = m_new


def _mla_prompt_body(q_ref, kv_ref, wuv_ref, o_ref, m_ref, acc_ref, *, H, tq, RANK, V, scale, n_pad):
    qi = pl.program_id(1)
    q = q_ref[0].reshape(H * tq, q_ref.shape[-1])
    m_ref[...] = jnp.full_like(m_ref, NEG)
    acc_ref[...] = jnp.zeros_like(acc_ref)
    t_col = qi * tq + lax.broadcasted_iota(jnp.int32, (H * tq, 1), 0) % tq

    def step(k0, w):
        k = kv_ref[pl.ds(k0, w), :]
        k_idx = k0 + lax.broadcasted_iota(jnp.int32, (1, w), 1)
        s = jnp.where((k_idx <= t_col) & (k_idx >= n_pad), _dot_t(q, k) * scale, NEG)
        _softmax_step_wide(s, k[:, :RANK], m_ref, acc_ref)

    nb = (qi + 1) // 2

    def body(j, carry):
        step(pl.multiple_of(j * 2 * tq, 2 * tq), 2 * tq)
        return carry

    lax.fori_loop(0, nb, body, 0)

    @pl.when((qi + 1) % 2 == 1)
    def _():
        step(pl.multiple_of(nb * 2 * tq, tq), tq)

    acc = acc_ref[...]
    o = acc[:, :RANK] / acc[:, RANK:RANK + 1]
    for h in range(H):
        o_ref[:, h * V:(h + 1) * V] = jnp.dot(o[h * tq:(h + 1) * tq], wuv_ref[h], preferred_element_type=F32)


def _mla_prompt(qabs, rows, wuv_t, B, L, tq, scale, n_pad):
    H, RANK, V = wuv_t.shape
    E = rows.shape[1]
    nq = L // tq
    return pl.pallas_call(
        functools.partial(_mla_prompt_body, H=H, tq=tq, RANK=RANK, V=V, scale=scale, n_pad=n_pad),
        out_shape=jax.ShapeDtypeStruct((B * L, H * V), F32),
        grid=(B, nq),
        in_specs=[pl.BlockSpec((1, H, tq, E), lambda b, i: (b * nq + i, 0, 0, 0)),
                  pl.BlockSpec((L, E), lambda b, i: (b, 0)),
                  pl.BlockSpec((H, RANK, V), lambda b, i: (0, 0, 0))],
        out_specs=pl.BlockSpec((tq, H * V), lambda b, i: (b * nq + i, 0)),
        scratch_shapes=[pltpu.VMEM((H * tq, LANE), F32), pltpu.VMEM((H * tq, RANK + LANE), F32)],
        compiler_params=_cparams(("parallel", "arbitrary"), 40),
        name="mla_prompt",
    )(qabs, rows, wuv_t)


def _mla_sample_body(pt_ref, q_ref, new_ref, *refs, H, T, RANK, V, scale, PP):
    pages = refs[:PP]
    wuv_ref, o_ref, m_ref, l_ref, acc_ref = refs[PP:]
    g = pl.program_id(1)
    q = q_ref[0, :, 0].reshape(H * T, q_ref.shape[-1])

    @pl.when(g == 0)
    def _():
        kn = new_ref[...]
        t_idx = lax.broadcasted_iota(jnp.int32, (H * T, T), 0) % T
        s_idx = lax.broadcasted_iota(jnp.int32, (H * T, T), 1)
        s = jnp.where(s_idx <= t_idx, _dot_t(q, kn) * scale, NEG)
        m0 = jnp.max(s, axis=1, keepdims=True)
        p = jnp.exp(s - m0)
        m_ref[...] = m0
        l_ref[...] = jnp.sum(p, axis=1, keepdims=True)
        acc_ref[...] = jnp.dot(p, kn[:, :RANK], preferred_element_type=F32)

    kt = jnp.concatenate([pg[0, 0] for pg in pages], axis=1)
    _softmax_step(jnp.dot(q, kt, preferred_element_type=F32) * scale, kt[:RANK], m_ref, l_ref, acc_ref, trans_v=True)

    @pl.when(g == pl.num_programs(1) - 1)
    def _():
        o = acc_ref[...] / l_ref[...]
        for h in range(H):
            o_ref[:, h * V:(h + 1) * V] = jnp.dot(o[h * T:(h + 1) * T], wuv_ref[h], preferred_element_type=F32)


def _mla_sample(qabs_s, rows, row0, cache_t, layer, pt_flat, wuv_t, Bs, T, n_pages, scale):
    H, RANK, V = wuv_t.shape
    E = rows.shape[1]
    per_blk = qabs_s.shape[2]
    PAGE = cache_t.shape[3]
    PP = min(PAGES_PER_STEP, n_pages)
    G = n_pages // PP
    rb0 = row0 // T
    page_specs = [pl.BlockSpec((1, 1, E, PAGE),
                               functools.partial(lambda b, g, pt, kk: (layer, pt[b * n_pages + g * PP + kk], 0, 0), kk=kk))
                  for kk in range(PP)]
    grid_spec = pltpu.PrefetchScalarGridSpec(
        num_scalar_prefetch=1,
        grid=(Bs, G),
        in_specs=[pl.BlockSpec((1, H, 1, T, E), lambda b, g, pt: (b // per_blk, 0, b % per_blk, 0, 0)),
                  pl.BlockSpec((T, E), lambda b, g, pt: (rb0 + b, 0))] + page_specs +
                 [pl.BlockSpec((H, RANK, V), lambda b, g, pt: (0, 0, 0))],
        out_specs=pl.BlockSpec((T, H * V), lambda b, g, pt: (b, 0)),
        scratch_shapes=[pltpu.VMEM((H * T, 1), F32), pltpu.VMEM((H * T, 1), F32), pltpu.VMEM((H * T, RANK), F32)],
    )
    return pl.pallas_call(
        functools.partial(_mla_sample_body, H=H, T=T, RANK=RANK, V=V, scale=scale, PP=PP),
        out_shape=jax.ShapeDtypeStruct((Bs * T, H * V), F32),
        grid_spec=grid_spec,
        compiler_params=_cparams(("parallel", "arbitrary"), 40),
        name="mla_sample",
    )(pt_flat, qabs_s, rows, *([cache_t] * PP), wuv_t)


def _diff_finish(a0, l0, a1, l1, lq1, lk1, lq2, lk2, ng, lam_init):
    lam = (jnp.exp(jnp.sum(lq1 * lk1, axis=1, keepdims=True))
           - jnp.exp(jnp.sum(lq2 * lk2, axis=1, keepdims=True)) + lam_init)
    d = a0 / l0 - lam * (a1 / l1)
    return _rms_rows(d, ng) * (1.0 - lam_init)


def _stack_maps(q_ref, H, W, DQK):
    q = jnp.concatenate([q_ref[:, h * W:(h + 1) * W] for h in range(H)], axis=0)
    first_map = lax.broadcasted_iota(jnp.int32, q.shape, 1) < DQK
    return jnp.concatenate([jnp.where(first_map, q, 0.0), jnp.where(first_map, 0.0, q)], axis=0)


def _diff_out(o_ref, a, l, lam_refs, ng_ref, H, n, DV, lam_init):
    HT = H * n
    d = _diff_finish(a[:HT], l[:HT], a[HT:], l[HT:], *[r[...] for r in lam_refs], ng_ref[...], lam_init)
    for h in range(H):
        o_ref[:, h * DV:(h + 1) * DV] = d[h * n:(h + 1) * n]


def _diff_prompt_body(q_ref, k_ref, v_ref, slope_ref, lq1_ref, lk1_ref, lq2_ref, lk2_ref, ng_ref, o_ref,
                      m_ref, a_ref, *, H, tq, DQK, DV, scale, n_pad, lam_init):
    qi = pl.program_id(1)
    qs = _stack_maps(q_ref, H, 2 * DQK, DQK)
    m_ref[...] = jnp.full_like(m_ref, NEG)
    a_ref[...] = jnp.zeros_like(a_ref)
    t_col = (qi * tq + lax.broadcasted_iota(jnp.int32, (2 * H * tq, 1), 0) % tq).astype(F32)
    slope = slope_ref[...]

    def step(k0, w):
        k = k_ref[pl.ds(k0, w), :]
        v = v_ref[pl.ds(k0, w), :]
        k_idx = k0 + lax.broadcasted_iota(jnp.int32, (1, w), 1)
        dist = t_col - k_idx.astype(F32)
        s = jnp.where((dist >= 0.0) & (k_idx >= n_pad), _dot_t(qs, k) * scale - slope * dist, NEG)
        _softmax_step_wide(s, v, m_ref, a_ref)

    nb = (qi + 1) // 2

    def body(j, carry):
        step(pl.multiple_of(j * 2 * tq, 2 * tq), 2 * tq)
        return carry

    lax.fori_loop(0, nb, body, 0)

    @pl.when((qi + 1) % 2 == 1)
    def _():
        step(pl.multiple_of(nb * 2 * tq, tq), tq)

    acc = a_ref[...]
    _diff_out(o_ref, acc[:, :DV], acc[:, DV:DV + 1], (lq1_ref, lk1_ref, lq2_ref, lk2_ref), ng_ref, H, tq, DV,
              lam_init)


def _diff_prompt(z, q_blk, k_blk, v_blk, slopes_col, lam_vecs, ng, B, L, tq, H, DQK, DV, scale, n_pad, lam_init):
    nq = L // tq
    W = 2 * DQK
    M = 2 * H * tq

    def cmap(b, i):
        return (0, 0)

    small = [pl.BlockSpec((1, DQK), cmap)] * 4 + [pl.BlockSpec((1, DV), cmap)]
    return pl.pallas_call(
        functools.partial(_diff_prompt_body, H=H, tq=tq, DQK=DQK, DV=DV, scale=scale, n_pad=n_pad, lam_init=lam_init),
        out_shape=jax.ShapeDtypeStruct((B * L, H * DV), F32),
        grid=(B, nq),
        in_specs=[pl.BlockSpec((tq, H * W), lambda b, i: (b * nq + i, q_blk)),
                  pl.BlockSpec((L, W), lambda b, i: (b, k_blk)),
                  pl.BlockSpec((L, DV), lambda b, i: (b, v_blk)),
                  pl.BlockSpec((M, 1), cmap)] + small,
        out_specs=pl.BlockSpec((tq, H * DV), lambda b, i: (b * nq + i, 0)),
        scratch_shapes=[pltpu.VMEM((M, LANE), F32), pltpu.VMEM((M, DV + LANE), F32)],
        compiler_params=_cparams(("parallel", "arbitrary"), 48),
        name="diff_prompt",
    )(z, z, z, slopes_col, *lam_vecs, ng.reshape(1, DV))


def _diff_sample_body(pt_ref, q_ref, kn_ref, vn_ref, *refs, H, T, DQK, DV, PAGE, past_len, scale, PP, lam_init):
    kpages = refs[:PP]
    vpages = refs[PP:2 * PP]
    slope_ref, lq1_ref, lk1_ref, lq2_ref, lk2_ref, ng_ref, o_ref, m_ref, l_ref, a_ref = refs[2 * PP:]
    g = pl.program_id(1)
    M = 2 * H * T
    qs = _stack_maps(q_ref, H, 2 * DQK, DQK)
    slope = slope_ref[...]
    t_col = lax.broadcasted_iota(jnp.int32, (M, 1), 0) % T

    @pl.when(g == 0)
    def _():
        s_idx = lax.broadcasted_iota(jnp.int32, (1, T), 1)
        dist = (t_col - s_idx).astype(F32)
        s = jnp.where(dist >= 0.0, _dot_t(qs, kn_ref[...]) * scale - slope * dist, NEG)
        m0 = jnp.max(s, axis=1, keepdims=True)
        p = jnp.exp(s - m0)
        m_ref[...] = m0
        l_ref[...] = jnp.sum(p, axis=1, keepdims=True)
        a_ref[...] = jnp.dot(p, vn_ref[...], preferred_element_type=F32)

    k_all = jnp.concatenate([kp[0, 0] for kp in kpages], axis=0)
    v_all = jnp.concatenate([vp[0, 0] for vp in vpages], axis=0)
    k_pos = g * (PP * PAGE) + lax.broadcasted_iota(jnp.int32, (1, PP * PAGE), 1)
    dist = (past_len + t_col).astype(F32) - k_pos.astype(F32)
    _softmax_step(_dot_t(qs, k_all) * scale - slope * dist, v_all, m_ref, l_ref, a_ref)

    @pl.when(g == pl.num_programs(1) - 1)
    def _():
        _diff_out(o_ref, a_ref[...], l_ref[...], (lq1_ref, lk1_ref, lq2_ref, lk2_ref), ng_ref, H, T, DV, lam_init)


def _diff_sample(z, row0, q_blk, k_blk, v_blk, cache_dk, cache_dv, layer, pt_flat, slopes_col, lam_vecs, ng,
                 Bs, T, H, DQK, DV, n_pages, scale, lam_init):
    W = 2 * DQK
    M = 2 * H * T
    PAGE = cache_dk.shape[2]
    PP = min(PAGES_PER_STEP, n_pages)
    G = n_pages // PP
    rb0 = row0 // T

    def pmap(kk):
        return functools.partial(lambda b, g, pt, kk: (layer, pt[b * n_pages + g * PP + kk], 0, 0), kk=kk)

    kspecs = [pl.BlockSpec((1, 1, PAGE, W), pmap(kk)) for kk in range(PP)]
    vspecs = [pl.BlockSpec((1, 1, PAGE, DV), pmap(kk)) for kk in range(PP)]

    def cmap(b, g, pt):
        return (0, 0)

    small = [pl.BlockSpec((M, 1), cmap)] + [pl.BlockSpec((1, DQK), cmap)] * 4 + [pl.BlockSpec((1, DV), cmap)]
    grid_spec = pltpu.PrefetchScalarGridSpec(
        num_scalar_prefetch=1,
        grid=(Bs, G),
        in_specs=[pl.BlockSpec((T, H * W), lambda b, g, pt: (rb0 + b, q_blk)),
                  pl.BlockSpec((T, W), lambda b, g, pt: (rb0 + b, k_blk)),
                  pl.BlockSpec((T, DV), lambda b, g, pt: (rb0 + b, v_blk))] + kspecs + vspecs + small,
        out_specs=pl.BlockSpec((T, H * DV), lambda b, g, pt: (b, 0)),
        scratch_shapes=[pltpu.VMEM((M, 1), F32), pltpu.VMEM((M, 1), F32), pltpu.VMEM((M, DV), F32)],
    )
    return pl.pallas_call(
        functools.partial(_diff_sample_body, H=H, T=T, DQK=DQK, DV=DV, PAGE=PAGE, past_len=n_pages * PAGE,
                          scale=scale, PP=PP, lam_init=lam_init),
        out_shape=jax.ShapeDtypeStruct((Bs * T, H * DV), F32),
        grid_spec=grid_spec,
        compiler_params=_cparams(("parallel", "arbitrary"), 40),
        name="diff_sample",
    )(pt_flat, z, z, z, *([cache_dk] * PP), *([cache_dv] * PP), slopes_col, *lam_vecs, ng.reshape(1, DV))


def _log_sigmoid(x):
    return jnp.minimum(x, 0.0) - jnp.log1p(jnp.exp(-jnp.abs(x)))


def _mlstm_body(q_ref, k_ref, v_ref, og_ref, gc_ref, gr_ref, bi_ref, bf_ref, ng_ref, c0_ref, n0_ref, m0_ref,
                h_ref, c_ref, n_ref, m_ref, *, NH, DH, Lc):
    @pl.when(pl.program_id(1) == 0)
    def _():
        c_ref[...] = c0_ref[...]
        n_ref[...] = n0_ref[...]
        m_ref[...] = m0_ref[...]

    row = lax.broadcasted_iota(jnp.int32, (Lc, Lc), 0)
    col = lax.broadcasted_iota(jnp.int32, (Lc, Lc), 1)
    tril = col <= row
    valid_c = gc_ref[:, 2 * NH:2 * NH + 1] > 0.0
    valid_r = gr_ref[0, 2 * NH:2 * NH + 1, :] > 0.0
    for hh in range(NH):
        sl = slice(hh * DH, (hh + 1) * DH)
        q = q_ref[:, sl]
        k = k_ref[:, sl] * (DH ** -0.5)
        v = v_ref[:, sl]
        b_i = bi_ref[:, hh:hh + 1]
        b_f = bf_ref[:, hh:hh + 1]
        i_c = jnp.where(valid_c, gc_ref[:, hh:hh + 1] + b_i, NEG)
        i_r = jnp.where(valid_r, gr_ref[0, hh:hh + 1, :] + b_i, NEG)
        lf_c = jnp.where(valid_c, _log_sigmoid(gc_ref[:, NH + hh:NH + hh + 1] + b_f), 0.0)
        lf_r = jnp.where(valid_r, _log_sigmoid(gr_ref[0, NH + hh:NH + hh + 1, :] + b_f), 0.0)
        b_c = jnp.sum(jnp.where(tril, lf_r, 0.0), axis=1, keepdims=True)
        b_r = jnp.sum(jnp.where(row <= col, lf_c, 0.0), axis=0, keepdims=True)
        C = c_ref[0, hh]
        n = n_ref[0, hh:hh + 1, :]
        m = m_ref[0, :, hh:hh + 1]
        D = jnp.where(tril, b_c - b_r + i_r, NEG)
        g_prev = b_c + m
        m_t = jnp.maximum(g_prev, jnp.max(D, axis=1, keepdims=True))
        a_prev = jnp.exp(g_prev - m_t)
        wqk = jnp.exp(D - m_t) * _dot_t(q, k)
        num = a_prev * _dot_t(q, C) + jnp.dot(wqk, v, preferred_element_type=F32)
        den = a_prev * jnp.sum(q * n, axis=1, keepdims=True) + jnp.sum(wqk, axis=1, keepdims=True)
        hv = num / jnp.maximum(jnp.abs(den), jnp.exp(-m_t))
        m_new = m_t[Lc - 1:Lc, :]
        b_last = b_c[Lc - 1:Lc, :]
        w_end = jnp.exp(b_last - b_c + i_c - m_new)
        a_end = jnp.exp(b_last + m - m_new)
        c_ref[0, hh] = a_end * C + lax.dot_general(w_end * v, k, (((0,), (0,)), ((), ())),
                                                   preferred_element_type=F32)
        n_ref[0, hh:hh + 1, :] = a_end * n + jnp.sum(w_end * k, axis=0, keepdims=True)
        m_ref[0, :, hh:hh + 1] = m_new
        hn = _rms_rows(hv, ng_ref[hh:hh + 1, :])
        h_ref[:, sl] = jax.nn.sigmoid(og_ref[:, sl]) * hn


def _mlstm(z, row0, nb, L, Lc, qkv_blk, gates_c, gates_r, b_i, b_f, ng, C0, n0, m0):
    NH, DH = ng.shape
    nc = L // Lc
    rb0 = row0 // Lc
    G = gates_c.shape[1]
    Wd = NH * DH

    def zspec(cb):
        return pl.BlockSpec((Lc, Wd), functools.partial(lambda b, c, cb: (rb0 + b * nc + c, cb), cb=cb))

    st_specs = [pl.BlockSpec((1, NH, DH, DH), lambda b, c: (b, 0, 0, 0)),
                pl.BlockSpec((1, NH, DH), lambda b, c: (b, 0, 0)),
                pl.BlockSpec((1, 1, NH), lambda b, c: (b, 0, 0))]
    return pl.pallas_call(
        functools.partial(_mlstm_body, NH=NH, DH=DH, Lc=Lc),
        out_shape=(jax.ShapeDtypeStruct((nb * L, Wd), F32),
                   jax.ShapeDtypeStruct((nb, NH, DH, DH), F32),
                   jax.ShapeDtypeStruct((nb, NH, DH), F32),
                   jax.ShapeDtypeStruct((nb, 1, NH), F32)),
        grid=(nb, nc),
        in_specs=[zspec(qkv_blk), zspec(qkv_blk + 1), zspec(qkv_blk + 2), zspec(qkv_blk + 3),
                  pl.BlockSpec((Lc, G), lambda b, c: (b * nc + c, 0)),
                  pl.BlockSpec((1, G, Lc), lambda b, c: (b * nc + c, 0, 0)),
                  pl.BlockSpec((1, NH), lambda b, c: (0, 0)),
                  pl.BlockSpec((1, NH), lambda b, c: (0, 0)),
                  pl.BlockSpec((NH, DH), lambda b, c: (0, 0))] + st_specs,
        out_specs=(pl.BlockSpec((Lc, Wd), lambda b, c: (b * nc + c, 0)), *st_specs),
        compiler_params=_cparams(("parallel", "arbitrary"), 32),
        name="mlstm",
    )(z, z, z, z, gates_c, gates_r, b_i.reshape(1, NH), b_f.reshape(1, NH), ng, C0, n0, m0.reshape(nb, 1, NH))


def _round_up(n, m):
    return (n + m - 1) // m * m


def kernel(x_prompt, x_sample, cache_mla, cache_dk, cache_dv, state_conv, state_mlstm_C, state_mlstm_n, state_mlstm_m, page_table, meta_tokens, norm_mix_g, norm_mlp_g, norm_final_g, w_in, conv_w, conv_b, conv_ln_g, conv_ln_b, mla_kv_g, mla_w_uk, mla_w_uv, mlstm_b_i, mlstm_b_f, mlstm_norm_g, diff_lq1, diff_lk1, diff_lq2, diff_lk2, diff_norm_g, w_branch, w_out, mlp_w1, mlp_w2):
    B, S, D = x_prompt.shape
    Bs, T, _ = x_sample.shape
    DEPTH = w_in.shape[0]
    n_meta = meta_tokens.shape[0]
    BLK = N_META_BLOCK
    pad = BLK - n_meta
    L = S + BLK
    Tp, Ts = B * L, Bs * T
    R = Tp + Ts
    C = conv_w.shape[2]
    CW = conv_w.shape[1]
    RANK, H, NOPE = mla_w_uk.shape[1:]
    V = mla_w_uv.shape[3]
    E = cache_mla.shape[3]
    ROPE = E - RANK
    NH, DH = mlstm_norm_g.shape[1:]
    DQK = diff_lq1.shape[1]
    DV = diff_norm_g.shape[1]
    Cb = w_branch.shape[2]
    HD = Cb // DV
    FF = mlp_w1.shape[2]
    n_pages = page_table.shape[1]
    PAGE = cache_mla.shape[2]
    past_len = n_pages * PAGE
    mla_scale = (NOPE + ROPE) ** -0.5
    df_scale = DQK ** -0.5

    GP = min(512, C)
    n_q, n_kv = H * (NOPE + ROPE), RANK + ROPE
    n_ml, n_dq, n_dk = NH * DH, HD * 2 * DQK, 2 * DQK
    off_a = 0
    off_b = off_a + 2 * C
    w_b = _round_up(n_q + n_kv, GP)
    off_c = off_b + w_b
    off_d = off_c + 4 * n_ml
    off_e = off_d + n_dq
    w_e = _round_up(n_dk + DV + 2 * NH, GP)
    off_g = off_e + w_e
    Nz = off_g + N_BRANCH * D
    assert off_b % w_b == 0 and off_c % n_ml == 0 and off_d % n_dq == 0 and off_e % n_dk == 0 and n_dk == DV
    splits = (2 * C, n_q, n_kv, 3 * n_ml, n_ml, 2 * NH, n_dq, n_dk, DV, N_BRANCH * D)
    src = [0]
    for n in splits:
        src.append(src[-1] + n)

    def pack_w(w):
        def cols(i):
            return w[:, src[i]:src[i + 1]]
        zb = jnp.zeros((D, w_b - n_q - n_kv), F32)
        ze = jnp.zeros((D, w_e - n_dk - DV - 2 * NH), F32)
        return jnp.concatenate([cols(0), cols(1), cols(2), zb, cols(3), cols(4), cols(6), cols(7), cols(8), cols(5),
                                ze, cols(9)], axis=1)

    off_if = off_e + n_dk + DV

    bm_big = _divisor(R, 1280, SUBLANE)
    bm_mid = _divisor(R, 640, SUBLANE)
    bn_in = _divisor(Nz, 512, GP)
    assert off_g % bn_in == 0
    bn_merge = _divisor(D, 256, LANE)
    bn_out = _divisor(D, 512, LANE)
    bf = _divisor(FF, 512, LANE)
    tq = BLK
    tp = _divisor(math.gcd(Tp, Ts), BLK, SUBLANE)
    assert tp == tq and tp % T == 0 and L % ML_CHUNK == 0 and L % tq == 0

    dt = x_prompt.dtype
    x_full = jnp.concatenate([jnp.zeros((B, pad, D), dt), jnp.broadcast_to(meta_tokens.astype(dt), (B, n_meta, D)),
                              x_prompt], axis=1)
    h = jnp.concatenate([x_full.reshape(Tp, D), x_sample.reshape(Ts, D)], axis=0)
    pos_p = jnp.arange(L, dtype=jnp.int32) - pad
    valid_p = (pos_p >= 0).astype(F32)
    pos_all = jnp.concatenate([jnp.tile(pos_p, B), jnp.tile(past_len + jnp.arange(T, dtype=jnp.int32), Bs)])
    row_mask = jnp.concatenate([jnp.tile(valid_p, B), jnp.ones((Ts,), F32)]).reshape(R, 1)
    half = ROPE // 2
    freq = ROPE_THETA ** (-jnp.arange(half, dtype=F32) / half)
    ang = pos_all.astype(F32)[:, None] * freq
    cos, sin = jnp.cos(ang), jnp.sin(ang)
    rope_cs = jnp.concatenate([cos, cos], axis=1)
    rope_sn = jnp.concatenate([-sin, sin], axis=1)
    slopes = 2.0 ** (-8.0 * jnp.arange(1, HD + 1, dtype=F32) / HD)
    slopes_p = jnp.tile(jnp.repeat(slopes, tq), 2).reshape(2 * HD * tq, 1)
    slopes_s = jnp.tile(jnp.repeat(slopes, T), 2).reshape(2 * HD * T, 1)
    pt_flat = page_table.reshape(-1).astype(jnp.int32)
    cache_mla_t = jnp.swapaxes(cache_mla, 2, 3)
    hist_p = jnp.zeros((B, CONV_HIST_ROWS, C), F32)
    zeros_C = jnp.zeros((B, NH, DH, DH), F32)
    zeros_n = jnp.zeros((B, NH, DH), F32)
    zeros_m = jnp.zeros((B, NH), F32)

    def gate_views(zif, valid, n_chunks, Lc):
        gc = jnp.concatenate([zif, valid[:, None]], axis=1)
        gr = gc.reshape(n_chunks, Lc, 2 * NH + 1).transpose(0, 2, 1)
        return gc, gr

    outs = [[] for _ in range(14)]
    for l in range(DEPTH):
        lam_init = 0.8 - 0.6 * math.exp(-0.3 * l)
        z = _inproj(h, norm_mix_g[l], pack_w(w_in[l]), bm_big, bn_in)

        a_p, u_p = _conv(z, 0, B, L, tq, hist_p, conv_w[l], conv_b[l], conv_ln_g[l], conv_ln_b[l], C)
        hist_s = jnp.pad(state_conv[l], ((0, 0), (CONV_HIST_ROWS - (CW - 1), 0), (0, 0)))
        a_s, u_s = _conv(z, Tp, Bs, T, T, hist_s, conv_w[l], conv_b[l], conv_ln_g[l], conv_ln_b[l], C)

        wuk_t = jnp.transpose(mla_w_uk[l], (1, 2, 0))
        wuv_t = jnp.transpose(mla_w_uv[l], (1, 0, 2))
        qabs, rows = _mla_prep(z, off_b // w_b, w_b, tp, rope_cs, rope_sn, wuk_t, mla_kv_g[l], n_q)
        b_p = _mla_prompt(qabs, rows, wuv_t, B, L, tq, mla_scale, pad)
        qabs_s = qabs[Tp // tp:].reshape(Ts // tp, H, tp // T, T, E)
        b_s = _mla_sample(qabs_s, rows, Tp, cache_mla_t, l, pt_flat, wuv_t, Bs, T, n_pages, mla_scale)

        zif = z[:, off_if:off_if + 2 * NH]
        gc_p, gr_p = gate_views(zif[:Tp], jnp.tile(valid_p, B), Tp // ML_CHUNK, ML_CHUNK)
        gc_s, gr_s = gate_views(zif[Tp:], jnp.ones((Ts,), F32), Bs, T)
        c_p, C_p, n_p, m_p = _mlstm(z, 0, B, L, ML_CHUNK, off_c // n_ml, gc_p, gr_p, mlstm_b_i[l], mlstm_b_f[l],
                                    mlstm_norm_g[l], zeros_C, zeros_n, zeros_m)
        c_s, C_s, n_s, m_s = _mlstm(z, Tp, Bs, T, T, off_c // n_ml, gc_s, gr_s, mlstm_b_i[l], mlstm_b_f[l],
                                    mlstm_norm_g[l], state_mlstm_C[l], state_mlstm_n[l], state_mlstm_m[l])

        lam_vecs = [v[l].reshape(1, DQK) for v in (diff_lq1, diff_lk1, diff_lq2, diff_lk2)]
        d_p = _diff_prompt(z, off_d // n_dq, off_e // n_dk, off_e // n_dk + 1, slopes_p, lam_vecs, diff_norm_g[l],
                           B, L, tq, HD, DQK, DV, df_scale, pad, lam_init)
        d_s = _diff_sample(z, Tp, off_d // n_dq, off_e // n_dk, off_e // n_dk + 1, cache_dk, cache_dv, l, pt_flat,
                           slopes_s, lam_vecs, diff_norm_g[l], Bs, T, HD, DQK, DV, n_pages, df_scale, lam_init)

        branches = [jnp.concatenate([p, s], axis=0) for p, s in ((a_p, a_s), (b_p, b_s), (c_p, c_s), (d_p, d_s))]
        mix = _merge(z, off_g, branches, w_branch[l], bm_mid, bn_merge)
        h = _mm_res(mix, w_out[l], h, bm_big, bn_out)
        h = _mlp(h, norm_mlp_g[l], mlp_w1[l], mlp_w2[l], row_mask, bm_mid, bf)

        dk_all = z[:, off_e:off_e + n_dk]
        dv_all = z[:, off_e + n_dk:off_e + n_dk + DV]
        layer_out = (
            rows[:Tp].reshape(B, L, E)[:, pad:], rows[Tp:].reshape(Bs, T, E),
            dk_all[:Tp].reshape(B, L, n_dk)[:, pad:], dk_all[Tp:].reshape(Bs, T, n_dk),
            dv_all[:Tp].reshape(B, L, DV)[:, pad:], dv_all[Tp:].reshape(Bs, T, DV),
            u_p.reshape(B, L, C)[:, L - (CW - 1):],
            jnp.concatenate([state_conv[l][:, T:], u_s.reshape(Bs, T, C)], axis=1),
            C_p, C_s, n_p, n_s, m_p.reshape(B, NH), m_s.reshape(Bs, NH))
        for o, v in zip(outs, layer_out):
            o.append(v)

    y = _final_norm(h, norm_final_g, _divisor(R, 512, SUBLANE))
    y_prompt = y[:Tp].reshape(B, L, D)[:, BLK:]
    y_sample = y[Tp:].reshape(Bs, T, D)
    return (y_prompt, y_sample) + tuple(jnp.stack(o, axis=0) for o in outs)
```

```python
import functools
import math

import jax
import jax.numpy as jnp
from jax import lax
from jax.experimental import pallas as pl
from jax.experimental.pallas import tpu as pltpu

F32 = jnp.float32

N_META_BLOCK = 128
EPS = 1e-6
NEG = -1e30
ROPE_THETA = 10000.0
ML_CHUNK = 64
N_BRANCH = 4

LANE = 128
SUBLANE = 8
VMEM_PHYSICAL_BYTES = 64 * 1024 * 1024
CONV_HIST_ROWS = 32
PAGES_PER_STEP = 64


def _cparams(sem, vmem_mib):
    return pltpu.CompilerParams(dimension_semantics=sem,
                                vmem_limit_bytes=min(int(vmem_mib * 1024 * 1024), VMEM_PHYSICAL_BYTES - (4 << 20)))


def _divisor(n, pref, mult):
    if n <= pref:
        return n
    d = (pref // mult) * mult
    while d >= mult:
        if n % d == 0:
            return d
        d -= mult
    raise ValueError(f"no block for {n} (pref {pref}, multiple of {mult})")


def _rms_rows(x, g):
    ms = jnp.mean(x * x, axis=-1, keepdims=True)
    return x * lax.rsqrt(ms + EPS) * g


def _norm_into(x_ref, g_ref, xn_ref, chunk):
    g = g_ref[...]

    def body(c, carry):
        r = pl.multiple_of(c * chunk, chunk)
        xn_ref[pl.ds(r, chunk), :] = _rms_rows(x_ref[pl.ds(r, chunk), :], g).astype(xn_ref.dtype)
        return carry

    lax.fori_loop(0, x_ref.shape[0] // chunk, body, 0)


def _inproj_body(x_ref, g_ref, w_ref, o_ref, xn_ref, *, chunk):
    @pl.when(pl.program_id(1) == 0)
    def _():
        _norm_into(x_ref, g_ref, xn_ref, chunk)

    o_ref[...] = jnp.dot(xn_ref[...], w_ref[...], preferred_element_type=F32)


def _inproj(h, g, w_all, layer, bm, bn):
    R, D = h.shape
    N = w_all.shape[2]
    chunk = _divisor(bm, 64, SUBLANE)
    vmem = (2 * bm * D * 4 + bm * D * 2 + 2 * D * bn * 2 + 2 * bm * bn * 4) / 2**20 + 8
    return pl.pallas_call(
        functools.partial(_inproj_body, chunk=chunk),
        out_shape=jax.ShapeDtypeStruct((R, N), F32),
        grid=(R // bm, N // bn),
        in_specs=[pl.BlockSpec((bm, D), lambda i, j: (i, 0)),
                  pl.BlockSpec((1, D), lambda i, j: (0, 0)),
                  pl.BlockSpec((None, D, bn), lambda i, j: (layer, 0, j))],
        out_specs=pl.BlockSpec((bm, bn), lambda i, j: (i, j)),
        scratch_shapes=[pltpu.VMEM((bm, D), w_all.dtype)],
        compiler_params=_cparams(("parallel", "arbitrary"), vmem),
        name="inproj",
    )(h, g.reshape(1, D), w_all)


def _merge_body(*refs, n_prompt_blocks):
    gp = refs[0:N_BRANCH]
    br_p = refs[N_BRANCH:2 * N_BRANCH]
    br_s = refs[2 * N_BRANCH:3 * N_BRANCH]
    wb_ref, o_ref = refs[3 * N_BRANCH], refs[3 * N_BRANCH + 1]

    def mix(br):
        acc = None
        for k in range(N_BRANCH):
            p = jnp.dot(br[k][...], wb_ref[k], preferred_element_type=F32)
            t = jax.nn.sigmoid(gp[k][...]) * p
            acc = t if acc is None else acc + t
        o_ref[...] = acc

    is_prompt = pl.program_id(0) < n_prompt_blocks

    @pl.when(is_prompt)
    def _():
        mix(br_p)

    @pl.when(jnp.logical_not(is_prompt))
    def _():
        mix(br_s)


def _merge(z, gate_off, br_p, br_s, wb_all, layer, bm, bn):
    R = z.shape[0]
    _, _, Cb, D = wb_all.shape
    n_p = br_p[0].shape[0] // bm
    gspecs = [pl.BlockSpec((bm, bn), functools.partial(lambda i, j, o: (i, o + j), o=(gate_off + k * D) // bn))
              for k in range(N_BRANCH)]
    pspecs = [pl.BlockSpec((bm, Cb), lambda i, j: (jnp.minimum(i, n_p - 1), 0)) for _ in range(N_BRANCH)]
    sspecs = [pl.BlockSpec((bm, Cb), lambda i, j: (jnp.maximum(i - n_p, 0), 0)) for _ in range(N_BRANCH)]
    vmem = (2 * N_BRANCH * bm * bn + 4 * N_BRANCH * bm * Cb + 2 * N_BRANCH * Cb * bn + 4 * bm * bn) * 4 / 2**20 + 6
    return pl.pallas_call(
        functools.partial(_merge_body, n_prompt_blocks=n_p),
        out_shape=jax.ShapeDtypeStruct((R, D), F32),
        grid=(R // bm, D // bn),
        in_specs=gspecs + pspecs + sspecs + [pl.BlockSpec((None, N_BRANCH, Cb, bn), lambda i, j: (layer, 0, 0, j))],
        out_specs=pl.BlockSpec((bm, bn), lambda i, j: (i, j)),
        compiler_params=_cparams(("parallel", "arbitrary"), vmem),
        name="merge",
    )(*([z] * N_BRANCH), *br_p, *br_s, wb_all)


def _mm_res_body(x_ref, w_ref, r_ref, o_ref):
    o_ref[...] = r_ref[...] + jnp.dot(x_ref[...], w_ref[...], preferred_element_type=F32)


def _mm_res(x, w_all, layer, res, bm, bn):
    R, K = x.shape
    N = w_all.shape[2]
    vmem = (2 * bm * K + 2 * K * bn + 4 * bm * bn) * 4 / 2**20 + 6
    return pl.pallas_call(
        _mm_res_body,
        out_shape=jax.ShapeDtypeStruct((R, N), F32),
        grid=(R // bm, N // bn),
        in_specs=[pl.BlockSpec((bm, K), lambda i, j: (i, 0)),
                  pl.BlockSpec((None, K, bn), lambda i, j: (layer, 0, j)),
                  pl.BlockSpec((bm, bn), lambda i, j: (i, j))],
        out_specs=pl.BlockSpec((bm, bn), lambda i, j: (i, j)),
        compiler_params=_cparams(("parallel", "arbitrary"), vmem),
        name="outproj",
    )(x, w_all, res)


def _mlp_body(x_ref, g_ref, w1_ref, w2_ref, mk_ref, o_ref, xn_ref, hid_ref, *, chunk, nf):
    f = pl.program_id(1)
    n_chunks = x_ref.shape[0] // chunk
    bf16 = xn_ref.dtype

    def up(slot):
        hid = jnp.maximum(jnp.dot(xn_ref[...], w1_ref[...].astype(bf16), preferred_element_type=F32), 0.0)
        hid_ref[slot] = (hid * hid).astype(bf16)

    def down(slot):
        o_ref[...] += jnp.dot(hid_ref[slot], w2_ref[...].astype(bf16), preferred_element_type=F32)

    @pl.when(f == 0)
    def _():
        g = g_ref[...]

        def body(c, carry):
            r = pl.multiple_of(c * chunk, chunk)
            x = x_ref[pl.ds(r, chunk), :]
            xn_ref[pl.ds(r, chunk), :] = _rms_rows(x, g).astype(bf16)
            o_ref[pl.ds(r, chunk), :] = x
            return carry

        lax.fori_loop(0, n_chunks, body, 0)
        up(0)

    middle = (f > 0) & (f < nf)

    @pl.when(middle & (f % 2 == 1))
    def _():
        down(0)
        up(1)

    @pl.when(middle & (f % 2 == 0))
    def _():
        down(1)
        up(0)

    @pl.when(f == nf)
    def _():
        down((nf - 1) % 2)

        def body(c, carry):
            r = pl.multiple_of(c * chunk, chunk)
            o_ref[pl.ds(r, chunk), :] = o_ref[pl.ds(r, chunk), :] * mk_ref[pl.ds(r, chunk), :]
            return carry

        lax.fori_loop(0, n_chunks, body, 0)


def _mlp(h, g, w1_all, w2_all, layer, row_mask, bm, bf):
    R, D = h.shape
    FF = w1_all.shape[2]
    nf = FF // bf
    chunk = _divisor(bm, 64, SUBLANE)
    vmem = (4 * bm * D * 4 + bm * D * 2 + 2 * bm * bf * 2 + 4 * D * bf * 4 + 2 * bm * LANE * 4) / 2**20 + 6
    return pl.pallas_call(
        functools.partial(_mlp_body, chunk=chunk, nf=nf),
        out_shape=jax.ShapeDtypeStruct((R, D), F32),
        grid=(R // bm, nf + 1),
        in_specs=[pl.BlockSpec((bm, D), lambda i, f: (i, 0)),
                  pl.BlockSpec((1, D), lambda i, f: (0, 0)),
                  pl.BlockSpec((None, D, bf), lambda i, f: (layer, 0, jnp.minimum(f, nf - 1))),
                  pl.BlockSpec((None, bf, D), lambda i, f: (layer, jnp.maximum(f - 1, 0), 0)),
                  pl.BlockSpec((bm, 1), lambda i, f: (i, 0))],
        out_specs=pl.BlockSpec((bm, D), lambda i, f: (i, 0)),
        scratch_shapes=[pltpu.VMEM((bm, D), jnp.bfloat16), pltpu.VMEM((2, bm, bf), jnp.bfloat16)],
        compiler_params=_cparams(("parallel", "arbitrary"), vmem),
        name="mlp",
    )(h, g.reshape(1, D), w1_all, w2_all, row_mask)


def _final_norm_body(x_ref, g_ref, o_ref):
    o_ref[...] = _rms_rows(x_ref[...], g_ref[...])


def _final_norm(h, g, blk0, n_seq, seq_blocks, out_blocks, bm):
    D = h.shape[1]
    skip = seq_blocks - out_blocks
    return pl.pallas_call(
        _final_norm_body,
        out_shape=jax.ShapeDtypeStruct((n_seq * out_blocks * bm, D), F32),
        grid=(n_seq, out_blocks),
        in_specs=[pl.BlockSpec((bm, D), lambda s, i: (blk0 + s * seq_blocks + skip + i, 0)),
                  pl.BlockSpec((1, D), lambda s, i: (0, 0))],
        out_specs=pl.BlockSpec((bm, D), lambda s, i: (s * out_blocks + i, 0)),
        compiler_params=_cparams(("parallel", "arbitrary"), 16),
        name="final_norm",
    )(h, g.reshape(1, D))


def _conv_body(*refs, tb, C, CW, has_prev):
    if has_prev:
        ag_ref, prev_ref, hist_ref, cw_ref, cb_ref, lg_ref, lb_ref, a_ref, u_ref, ubuf, ybuf = refs
    else:
        ag_ref, hist_ref, cw_ref, cb_ref, lg_ref, lb_ref, a_ref, u_ref, ubuf, ybuf = refs
    HP = CONV_HIST_ROWS
    ag = ag_ref[...]
    u = ag[:, :C] * jax.nn.sigmoid(ag[:, C:])
    u_ref[...] = u
    ubuf[HP:, :] = u
    if has_prev:
        first = pl.program_id(1) == 0

        @pl.when(first)
        def _():
            ubuf[:HP, :] = hist_ref[0]

        @pl.when(jnp.logical_not(first))
        def _():
            p = prev_ref[...]
            ubuf[:HP, :] = p[:, :C] * jax.nn.sigmoid(p[:, C:])
    else:
        ubuf[:HP, :] = hist_ref[0]
    off = HP - (CW - 1)

    def cbody(cc, carry):
        c0 = pl.multiple_of(cc * LANE, LANE)
        acc = jnp.zeros((tb, LANE), F32)
        for w in range(CW):
            acc = acc + ubuf[pl.ds(off + w, tb), pl.ds(c0, LANE)] * cw_ref[pl.ds(w, 1), pl.ds(c0, LANE)]
        ybuf[:, pl.ds(c0, LANE)] = acc + cb_ref[:, pl.ds(c0, LANE)]
        return carry

    lax.fori_loop(0, C // LANE, cbody, 0)
    y = ybuf[...]
    yc = y - jnp.mean(y, axis=-1, keepdims=True)
    yn = yc * lax.rsqrt(jnp.mean(yc * yc, axis=-1, keepdims=True) + EPS) * lg_ref[...] + lb_ref[...]
    a_ref[...] = yn * jax.nn.sigmoid(yn)


def _conv(z, row0, nb, L, tb, hist, cw, cb, lg, lb, C):
    CW = cw.shape[0]
    HP = CONV_HIST_ROWS
    nt = L // tb
    has_prev = nt > 1
    cw_p = jnp.zeros((HP, C), F32).at[:CW].set(cw)
    rb0 = row0 // tb
    in_specs = [pl.BlockSpec((tb, 2 * C), lambda b, i: (rb0 + b * nt + i, 0))]
    args = [z]
    if has_prev:
        pb0 = row0 // HP
        in_specs.append(pl.BlockSpec((HP, 2 * C), lambda b, i: (jnp.maximum(pb0 + (b * L + i * tb) // HP - 1, 0), 0)))
        args.append(z)
    in_specs += [pl.BlockSpec((1, HP, C), lambda b, i: (b, 0, 0)),
                 pl.BlockSpec((HP, C), lambda b, i: (0, 0)),
                 pl.BlockSpec((1, C), lambda b, i: (0, 0)),
                 pl.BlockSpec((1, C), lambda b, i: (0, 0)),
                 pl.BlockSpec((1, C), lambda b, i: (0, 0))]
    args += [hist, cw_p, cb.reshape(1, C), lg.reshape(1, C), lb.reshape(1, C)]
    return pl.pallas_call(
        functools.partial(_conv_body, tb=tb, C=C, CW=CW, has_prev=has_prev),
        out_shape=(jax.ShapeDtypeStruct((nb * L, C), F32), jax.ShapeDtypeStruct((nb * L, C), F32)),
        grid=(nb, nt),
        in_specs=in_specs,
        out_specs=(pl.BlockSpec((tb, C), lambda b, i: (b * nt + i, 0)),
                   pl.BlockSpec((tb, C), lambda b, i: (b * nt + i, 0))),
        scratch_shapes=[pltpu.VMEM((HP + tb, C), F32), pltpu.VMEM((tb, C), F32)],
        compiler_params=_cparams(("parallel", "arbitrary"), 32),
        name="conv",
    )(*args)


def _mla_prep_body(zq_ref, cs_ref, sn_ref, wuk_ref, kvg_ref, qabs_ref, rows_ref, *, H, NOPE, ROPE, RANK, kv_off):
    cs = cs_ref[...]
    sn = sn_ref[...]
    half = ROPE // 2

    def rope(x):
        xs = jnp.concatenate([x[:, half:], x[:, :half]], axis=1)
        return x * cs + xs * sn

    for h in range(H):
        base = h * (NOPE + ROPE)
        qabs_ref[0, h, :, :RANK] = jnp.dot(zq_ref[:, base:base + NOPE], wuk_ref[h], preferred_element_type=F32)
        qabs_ref[0, h, :, RANK:] = rope(zq_ref[:, base + NOPE:base + NOPE + ROPE])
    rows_ref[:, :RANK] = _rms_rows(zq_ref[:, kv_off:kv_off + RANK], kvg_ref[...])
    rows_ref[:, RANK:] = rope(zq_ref[:, kv_off + RANK:kv_off + RANK + ROPE])


def _mla_prep(z, col_blk, wblk, tp, cs, sn, wuk_t, kvg, kv_off):
    R = z.shape[0]
    H, NOPE, RANK = wuk_t.shape
    ROPE = cs.shape[1]
    E = RANK + ROPE
    return pl.pallas_call(
        functools.partial(_mla_prep_body, H=H, NOPE=NOPE, ROPE=ROPE, RANK=RANK, kv_off=kv_off),
        out_shape=(jax.ShapeDtypeStruct((R // tp, H, tp, E), F32), jax.ShapeDtypeStruct((R, E), F32)),
        grid=(R // tp,),
        in_specs=[pl.BlockSpec((tp, wblk), lambda i: (i, col_blk)),
                  pl.BlockSpec((tp, ROPE), lambda i: (i, 0)),
                  pl.BlockSpec((tp, ROPE), lambda i: (i, 0)),
                  pl.BlockSpec((H, NOPE, RANK), lambda i: (0, 0, 0)),
                  pl.BlockSpec((1, RANK), lambda i: (0, 0))],
        out_specs=(pl.BlockSpec((1, H, tp, E), lambda i: (i, 0, 0, 0)),
                   pl.BlockSpec((tp, E), lambda i: (i, 0))),
        compiler_params=_cparams(("parallel",), 32),
        name="mla_prep",
    )(z, cs, sn, wuk_t, kvg.reshape(1, RANK))


def _dot_t(a, b):
    return lax.dot_general(a, b, (((1,), (1,)), ((), ())), preferred_element_type=F32)


def _softmax_step(s, v, m_ref, l_ref, acc_ref, trans_v=False):
    m_prev = m_ref[...]
    m_new = jnp.maximum(m_prev, jnp.max(s, axis=1, keepdims=True))
    alpha = jnp.exp(m_prev - m_new)
    p = jnp.exp(s - m_new)
    l_ref[...] = alpha * l_ref[...] + jnp.sum(p, axis=1, keepdims=True)
    pv = _dot_t(p, v) if trans_v else jnp.dot(p, v, preferred_element_type=F32)
    acc_ref[...] = alpha * acc_ref[...] + pv
    m_ref[...] = m_new


def _lanes(x, n):
    return x if n == 1 else jnp.concatenate([x] * n, axis=1)


def _softmax_step_wide(s, v, m_ref, acc_ref):
    w = s.shape[1]
    m_prev = m_ref[...]
    m_new = jnp.maximum(m_prev, jnp.max(s, axis=1, keepdims=True))
    alpha = jnp.exp(m_prev - m_new)
    p = jnp.exp(s - _lanes(m_new, w // LANE))
    v_ext = jnp.concatenate([v, jnp.ones((w, LANE), F32)], axis=1)
    acc_ref[...] = (_lanes(alpha, acc_ref.shape[1] // LANE) * acc_ref[...]
                    + jnp.dot(p, v_ext, preferred_element_type=F32))
    m_ref[...] = m_new


def _mla_prompt_body(q_ref, kv_ref, wuv_ref, o_ref, m_ref, acc_ref, *, H, tq, RANK, V, scale, n_pad):
    qi = pl.program_id(1)
    q = q_ref[0].reshape(H * tq, q_ref.shape[-1])
    m_ref[...] = jnp.full_like(m_ref, NEG)
    acc_ref[...] = jnp.zeros_like(acc_ref)
    t_col = qi * tq + lax.broadcasted_iota(jnp.int32, (H * tq, 1), 0) % tq

    def step(k0, w):
        k = kv_ref[pl.ds(k0, w), :]
        k_idx = k0 + lax.broadcasted_iota(jnp.int32, (1, w), 1)
        s = jnp.where((k_idx <= t_col) & (k_idx >= n_pad), _dot_t(q, k) * scale, NEG)
        _softmax_step_wide(s, k[:, :RANK], m_ref, acc_ref)

    nb = (qi + 1) // 2

    def body(j, carry):
        step(pl.multiple_of(j * 2 * tq, 2 * tq), 2 * tq)
        return carry

    lax.fori_loop(0, nb, body, 0)

    @pl.when((qi + 1) % 2 == 1)
    def _():
        step(pl.multiple_of(nb * 2 * tq, tq), tq)

    acc = acc_ref[...]
    o = acc[:, :RANK] / acc[:, RANK:RANK + 1]
    for h in range(H):
        o_ref[:, h * V:(h + 1) * V] = jnp.dot(o[h * tq:(h + 1) * tq], wuv_ref[h], preferred_element_type=F32)


def _mla_prompt(qabs, rows, wuv_t, B, L, tq, scale, n_pad):
    H, RANK, V = wuv_t.shape
    E = rows.shape[1]
    nq = L // tq
    return pl.pallas_call(
        functools.partial(_mla_prompt_body, H=H, tq=tq, RANK=RANK, V=V, scale=scale, n_pad=n_pad),
        out_shape=jax.ShapeDtypeStruct((B * L, H * V), F32),
        grid=(B, nq),
        in_specs=[pl.BlockSpec((1, H, tq, E), lambda b, i: (b * nq + i, 0, 0, 0)),
                  pl.BlockSpec((L, E), lambda b, i: (b, 0)),
                  pl.BlockSpec((H, RANK, V), lambda b, i: (0, 0, 0))],
        out_specs=pl.BlockSpec((tq, H * V), lambda b, i: (b * nq + i, 0)),
        scratch_shapes=[pltpu.VMEM((H * tq, LANE), F32), pltpu.VMEM((H * tq, RANK + LANE), F32)],
        compiler_params=_cparams(("parallel", "arbitrary"), 40),
        name="mla_prompt",
    )(qabs, rows, wuv_t)


def _mla_sample_body(pt_ref, q_ref, new_ref, *refs, H, T, RANK, V, scale, PP):
    pages = refs[:PP]
    wuv_ref, o_ref, m_ref, l_ref, acc_ref = refs[PP:]
    g = pl.program_id(1)
    q = q_ref[0, :, 0].reshape(H * T, q_ref.shape[-1])

    @pl.when(g == 0)
    def _():
        kn = new_ref[...]
        t_idx = lax.broadcasted_iota(jnp.int32, (H * T, T), 0) % T
        s_idx = lax.broadcasted_iota(jnp.int32, (H * T, T), 1)
        s = jnp.where(s_idx <= t_idx, _dot_t(q, kn) * scale, NEG)
        m0 = jnp.max(s, axis=1, keepdims=True)
        p = jnp.exp(s - m0)
        m_ref[...] = m0
        l_ref[...] = jnp.sum(p, axis=1, keepdims=True)
        acc_ref[...] = jnp.dot(p, kn[:, :RANK], preferred_element_type=F32)

    kt = jnp.concatenate([pg[0, 0] for pg in pages], axis=1)
    _softmax_step(jnp.dot(q, kt, preferred_element_type=F32) * scale, kt[:RANK], m_ref, l_ref, acc_ref, trans_v=True)

    @pl.when(g == pl.num_programs(1) - 1)
    def _():
        o = acc_ref[...] / l_ref[...]
        for h in range(H):
            o_ref[:, h * V:(h + 1) * V] = jnp.dot(o[h * T:(h + 1) * T], wuv_ref[h], preferred_element_type=F32)


def _mla_sample(qabs_s, rows, row0, cache_t, layer, pt_flat, wuv_t, Bs, T, n_pages, scale):
    H, RANK, V = wuv_t.shape
    E = rows.shape[1]
    per_blk = qabs_s.shape[2]
    PAGE = cache_t.shape[3]
    PP = min(PAGES_PER_STEP, n_pages)
    G = n_pages // PP
    rb0 = row0 // T
    page_specs = [pl.BlockSpec((1, 1, E, PAGE),
                               functools.partial(lambda b, g, pt, kk: (layer, pt[b * n_pages + g * PP + kk], 0, 0), kk=kk))
                  for kk in range(PP)]
    grid_spec = pltpu.PrefetchScalarGridSpec(
        num_scalar_prefetch=1,
        grid=(Bs, G),
        in_specs=[pl.BlockSpec((1, H, 1, T, E), lambda b, g, pt: (b // per_blk, 0, b % per_blk, 0, 0)),
                  pl.BlockSpec((T, E), lambda b, g, pt: (rb0 + b, 0))] + page_specs +
                 [pl.BlockSpec((H, RANK, V), lambda b, g, pt: (0, 0, 0))],
        out_specs=pl.BlockSpec((T, H * V), lambda b, g, pt: (b, 0)),
        scratch_shapes=[pltpu.VMEM((H * T, 1), F32), pltpu.VMEM((H * T, 1), F32), pltpu.VMEM((H * T, RANK), F32)],
    )
    return pl.pallas_call(
        functools.partial(_mla_sample_body, H=H, T=T, RANK=RANK, V=V, scale=scale, PP=PP),
        out_shape=jax.ShapeDtypeStruct((Bs * T, H * V), F32),
        grid_spec=grid_spec,
        compiler_params=_cparams(("parallel", "arbitrary"), 40),
        name="mla_sample",
    )(pt_flat, qabs_s, rows, *([cache_t] * PP), wuv_t)


def _diff_finish(a0, l0, a1, l1, lq1, lk1, lq2, lk2, ng, lam_init):
    lam = (jnp.exp(jnp.sum(lq1 * lk1, axis=1, keepdims=True))
           - jnp.exp(jnp.sum(lq2 * lk2, axis=1, keepdims=True)) + lam_init)
    d = a0 / l0 - lam * (a1 / l1)
    return _rms_rows(d, ng) * (1.0 - lam_init)


def _stack_maps(q_ref, H, W, DQK):
    q = jnp.concatenate([q_ref[:, h * W:(h + 1) * W] for h in range(H)], axis=0)
    first_map = lax.broadcasted_iota(jnp.int32, q.shape, 1) < DQK
    return jnp.concatenate([jnp.where(first_map, q, 0.0), jnp.where(first_map, 0.0, q)], axis=0)


def _diff_out(o_ref, a, l, lam_refs, ng_ref, H, n, DV, lam_init):
    HT = H * n
    d = _diff_finish(a[:HT], l[:HT], a[HT:], l[HT:], *[r[...] for r in lam_refs], ng_ref[...], lam_init)
    for h in range(H):
        o_ref[:, h * DV:(h + 1) * DV] = d[h * n:(h + 1) * n]


def _diff_prompt_body(q_ref, k_ref, v_ref, slope_ref, lq1_ref, lk1_ref, lq2_ref, lk2_ref, ng_ref, o_ref,
                      m_ref, a_ref, *, H, tq, DQK, DV, scale, n_pad, lam_init):
    qi = pl.program_id(1)
    qs = _stack_maps(q_ref, H, 2 * DQK, DQK)
    m_ref[...] = jnp.full_like(m_ref, NEG)
    a_ref[...] = jnp.zeros_like(a_ref)
    t_col = (qi * tq + lax.broadcasted_iota(jnp.int32, (2 * H * tq, 1), 0) % tq).astype(F32)
    slope = slope_ref[...]

    def step(k0, w):
        k = k_ref[pl.ds(k0, w), :]
        v = v_ref[pl.ds(k0, w), :]
        k_idx = k0 + lax.broadcasted_iota(jnp.int32, (1, w), 1)
        dist = t_col - k_idx.astype(F32)
        s = jnp.where((dist >= 0.0) & (k_idx >= n_pad), _dot_t(qs, k) * scale - slope * dist, NEG)
        _softmax_step_wide(s, v, m_ref, a_ref)

    nb = (qi + 1) // 2

    def body(j, carry):
        step(pl.multiple_of(j * 2 * tq, 2 * tq), 2 * tq)
        return carry

    lax.fori_loop(0, nb, body, 0)

    @pl.when((qi + 1) % 2 == 1)
    def _():
        step(pl.multiple_of(nb * 2 * tq, tq), tq)

    acc = a_ref[...]
    _diff_out(o_ref, acc[:, :DV], acc[:, DV:DV + 1], (lq1_ref, lk1_ref, lq2_ref, lk2_ref), ng_ref, H, tq, DV,
              lam_init)


def _diff_prompt(z, q_blk, k_blk, v_blk, slopes_col, lam_vecs, ng, B, L, tq, H, DQK, DV, scale, n_pad, lam_init):
    nq = L // tq
    W = 2 * DQK
    M = 2 * H * tq

    def cmap(b, i):
        return (0, 0)

    small = [pl.BlockSpec((1, DQK), cmap)] * 4 + [pl.BlockSpec((1, DV), cmap)]
    return pl.pallas_call(
        functools.partial(_diff_prompt_body, H=H, tq=tq, DQK=DQK, DV=DV, scale=scale, n_pad=n_pad, lam_init=lam_init),
        out_shape=jax.ShapeDtypeStruct((B * L, H * DV), F32),
        grid=(B, nq),
        in_specs=[pl.BlockSpec((tq, H * W), lambda b, i: (b * nq + i, q_blk)),
                  pl.BlockSpec((L, W), lambda b, i: (b, k_blk)),
                  pl.BlockSpec((L, DV), lambda b, i: (b, v_blk)),
                  pl.BlockSpec((M, 1), cmap)] + small,
        out_specs=pl.BlockSpec((tq, H * DV), lambda b, i: (b * nq + i, 0)),
        scratch_shapes=[pltpu.VMEM((M, LANE), F32), pltpu.VMEM((M, DV + LANE), F32)],
        compiler_params=_cparams(("parallel", "arbitrary"), 48),
        name="diff_prompt",
    )(z, z, z, slopes_col, *lam_vecs, ng.reshape(1, DV))


def _diff_sample_body(pt_ref, q_ref, kn_ref, vn_ref, *refs, H, T, DQK, DV, PAGE, past_len, scale, PP, lam_init):
    kpages = refs[:PP]
    vpages = refs[PP:2 * PP]
    slope_ref, lq1_ref, lk1_ref, lq2_ref, lk2_ref, ng_ref, o_ref, m_ref, l_ref, a_ref = refs[2 * PP:]
    g = pl.program_id(1)
    M = 2 * H * T
    qs = _stack_maps(q_ref, H, 2 * DQK, DQK)
    slope = slope_ref[...]
    t_col = lax.broadcasted_iota(jnp.int32, (M, 1), 0) % T

    @pl.when(g == 0)
    def _():
        s_idx = lax.broadcasted_iota(jnp.int32, (1, T), 1)
        dist = (t_col - s_idx).astype(F32)
        s = jnp.where(dist >= 0.0, _dot_t(qs, kn_ref[...]) * scale - slope * dist, NEG)
        m0 = jnp.max(s, axis=1, keepdims=True)
        p = jnp.exp(s - m0)
        m_ref[...] = m0
        l_ref[...] = jnp.sum(p, axis=1, keepdims=True)
        a_ref[...] = jnp.dot(p, vn_ref[...], preferred_element_type=F32)

    k_all = jnp.concatenate([kp[0, 0] for kp in kpages], axis=0)
    v_all = jnp.concatenate([vp[0, 0] for vp in vpages], axis=0)
    k_pos = g * (PP * PAGE) + lax.broadcasted_iota(jnp.int32, (1, PP * PAGE), 1)
    dist = (past_len + t_col).astype(F32) - k_pos.astype(F32)
    _softmax_step(_dot_t(qs, k_all) * scale - slope * dist, v_all, m_ref, l_ref, a_ref)

    @pl.when(g == pl.num_programs(1) - 1)
    def _():
        _diff_out(o_ref, a_ref[...], l_ref[...], (lq1_ref, lk1_ref, lq2_ref, lk2_ref), ng_ref, H, T, DV, lam_init)


def _diff_sample(z, row0, q_blk, k_blk, v_blk, cache_dk, cache_dv, layer, pt_flat, slopes_col, lam_vecs, ng,
                 Bs, T, H, DQK, DV, n_pages, scale, lam_init):
    W = 2 * DQK
    M = 2 * H * T
    PAGE = cache_dk.shape[2]
    PP = min(PAGES_PER_STEP, n_pages)
    G = n_pages // PP
    rb0 = row0 // T

    def pmap(kk):
        return functools.partial(lambda b, g, pt, kk: (layer, pt[b * n_pages + g * PP + kk], 0, 0), kk=kk)

    kspecs = [pl.BlockSpec((1, 1, PAGE, W), pmap(kk)) for kk in range(PP)]
    vspecs = [pl.BlockSpec((1, 1, PAGE, DV), pmap(kk)) for kk in range(PP)]

    def cmap(b, g, pt):
        return (0, 0)

    small = [pl.BlockSpec((M, 1), cmap)] + [pl.BlockSpec((1, DQK), cmap)] * 4 + [pl.BlockSpec((1, DV), cmap)]
    grid_spec = pltpu.PrefetchScalarGridSpec(
        num_scalar_prefetch=1,
        grid=(Bs, G),
        in_specs=[pl.BlockSpec((T, H * W), lambda b, g, pt: (rb0 + b, q_blk)),
                  pl.BlockSpec((T, W), lambda b, g, pt: (rb0 + b, k_blk)),
                  pl.BlockSpec((T, DV), lambda b, g, pt: (rb0 + b, v_blk))] + kspecs + vspecs + small,
        out_specs=pl.BlockSpec((T, H * DV), lambda b, g, pt: (b, 0)),
        scratch_shapes=[pltpu.VMEM((M, 1), F32), pltpu.VMEM((M, 1), F32), pltpu.VMEM((M, DV), F32)],
    )
    return pl.pallas_call(
        functools.partial(_diff_sample_body, H=H, T=T, DQK=DQK, DV=DV, PAGE=PAGE, past_len=n_pages * PAGE,
                          scale=scale, PP=PP, lam_init=lam_init),
        out_shape=jax.ShapeDtypeStruct((Bs * T, H * DV), F32),
        grid_spec=grid_spec,
        compiler_params=_cparams(("parallel", "arbitrary"), 40),
        name="diff_sample",
    )(pt_flat, z, z, z, *([cache_dk] * PP), *([cache_dv] * PP), slopes_col, *lam_vecs, ng.reshape(1, DV))


def _log_sigmoid(x):
    return jnp.minimum(x, 0.0) - jnp.log1p(jnp.exp(-jnp.abs(x)))


def _mlstm_body(q_ref, k_ref, v_ref, og_ref, gc_ref, gr_ref, bi_ref, bf_ref, ng_ref, c0_ref, n0_ref, m0_ref,
                h_ref, c_ref, n_ref, m_ref, *, NH, DH, Lc):
    @pl.when(pl.program_id(1) == 0)
    def _():
        c_ref[...] = c0_ref[...]
        n_ref[...] = n0_ref[...]
        m_ref[...] = m0_ref[...]

    row = lax.broadcasted_iota(jnp.int32, (Lc, Lc), 0)
    col = lax.broadcasted_iota(jnp.int32, (Lc, Lc), 1)
    tril = col <= row
    valid_c = gc_ref[:, 2 * NH:2 * NH + 1] > 0.0
    valid_r = gr_ref[0, 2 * NH:2 * NH + 1, :] > 0.0
    for hh in range(NH):
        sl = slice(hh * DH, (hh + 1) * DH)
        q = q_ref[:, sl]
        k = k_ref[:, sl] * (DH ** -0.5)
        v = v_ref[:, sl]
        b_i = bi_ref[:, hh:hh + 1]
        b_f = bf_ref[:, hh:hh + 1]
        i_c = jnp.where(valid_c, gc_ref[:, hh:hh + 1] + b_i, NEG)
        i_r = jnp.where(valid_r, gr_ref[0, hh:hh + 1, :] + b_i, NEG)
        lf_c = jnp.where(valid_c, _log_sigmoid(gc_ref[:, NH + hh:NH + hh + 1] + b_f), 0.0)
        lf_r = jnp.where(valid_r, _log_sigmoid(gr_ref[0, NH + hh:NH + hh + 1, :] + b_f), 0.0)
        b_c = jnp.sum(jnp.where(tril, lf_r, 0.0), axis=1, keepdims=True)
        b_r = jnp.sum(jnp.where(row <= col, lf_c, 0.0), axis=0, keepdims=True)
        C = c_ref[0, hh]
        n = n_ref[0, hh:hh + 1, :]
        m = m_ref[0, :, hh:hh + 1]
        D = jnp.where(tril, b_c - b_r + i_r, NEG)
        g_prev = b_c + m
        m_t = jnp.maximum(g_prev, jnp.max(D, axis=1, keepdims=True))
        a_prev = jnp.exp(g_prev - m_t)
        wqk = jnp.exp(D - m_t) * _dot_t(q, k)
        num = a_prev * _dot_t(q, C) + jnp.dot(wqk, v, preferred_element_type=F32)
        den = a_prev * jnp.sum(q * n, axis=1, keepdims=True) + jnp.sum(wqk, axis=1, keepdims=True)
        hv = num / jnp.maximum(jnp.abs(den), jnp.exp(-m_t))
        m_new = m_t[Lc - 1:Lc, :]
        b_last = b_c[Lc - 1:Lc, :]
        w_end = jnp.exp(b_last - b_c + i_c - m_new)
        a_end = jnp.exp(b_last + m - m_new)
        c_ref[0, hh] = a_end * C + lax.dot_general(w_end * v, k, (((0,), (0,)), ((), ())),
                                                   preferred_element_type=F32)
        n_ref[0, hh:hh + 1, :] = a_end * n + jnp.sum(w_end * k, axis=0, keepdims=True)
        m_ref[0, :, hh:hh + 1] = m_new
        hn = _rms_rows(hv, ng_ref[hh:hh + 1, :])
        h_ref[:, sl] = jax.nn.sigmoid(og_ref[:, sl]) * hn


def _mlstm(z, row0, nb, L, Lc, qkv_blk, gates_c, gates_r, b_i, b_f, ng, C0, n0, m0, layer):
    NH, DH = ng.shape
    nc = L // Lc
    rb0 = row0 // Lc
    G = gates_c.shape[1]
    Wd = NH * DH

    def zspec(cb):
        return pl.BlockSpec((Lc, Wd), functools.partial(lambda b, c, cb: (rb0 + b * nc + c, cb), cb=cb))

    st_specs = [pl.BlockSpec((1, NH, DH, DH), lambda b, c: (b, 0, 0, 0)),
                pl.BlockSpec((1, NH, DH), lambda b, c: (b, 0, 0)),
                pl.BlockSpec((1, 1, NH), lambda b, c: (b, 0, 0))]
    st_in_specs = [pl.BlockSpec((None, 1, NH, DH, DH), lambda b, c: (layer, b, 0, 0, 0)),
                   pl.BlockSpec((None, 1, NH, DH), lambda b, c: (layer, b, 0, 0)),
                   pl.BlockSpec((None, 1, 1, NH), lambda b, c: (layer, b, 0, 0))]
    return pl.pallas_call(
        functools.partial(_mlstm_body, NH=NH, DH=DH, Lc=Lc),
        out_shape=(jax.ShapeDtypeStruct((nb * L, Wd), F32),
                   jax.ShapeDtypeStruct((nb, NH, DH, DH), F32),
                   jax.ShapeDtypeStruct((nb, NH, DH), F32),
                   jax.ShapeDtypeStruct((nb, 1, NH), F32)),
        grid=(nb, nc),
        in_specs=[zspec(qkv_blk), zspec(qkv_blk + 1), zspec(qkv_blk + 2), zspec(qkv_blk + 3),
                  pl.BlockSpec((Lc, G), lambda b, c: (b * nc + c, 0)),
                  pl.BlockSpec((1, G, Lc), lambda b, c: (b * nc + c, 0, 0)),
                  pl.BlockSpec((1, NH), lambda b, c: (0, 0)),
                  pl.BlockSpec((1, NH), lambda b, c: (0, 0)),
                  pl.BlockSpec((NH, DH), lambda b, c: (0, 0))] + st_in_specs,
        out_specs=(pl.BlockSpec((Lc, Wd), lambda b, c: (b * nc + c, 0)), *st_specs),
        compiler_params=_cparams(("parallel", "arbitrary"), 32),
        name="mlstm",
    )(z, z, z, z, gates_c, gates_r, b_i.reshape(1, NH), b_f.reshape(1, NH), ng, C0, n0,
      m0.reshape(m0.shape[0], nb, 1, NH))


def _round_up(n, m):
    return (n + m - 1) // m * m


def kernel(x_prompt, x_sample, cache_mla, cache_dk, cache_dv, state_conv, state_mlstm_C, state_mlstm_n, state_mlstm_m, page_table, meta_tokens, norm_mix_g, norm_mlp_g, norm_final_g, w_in, conv_w, conv_b, conv_ln_g, conv_ln_b, mla_kv_g, mla_w_uk, mla_w_uv, mlstm_b_i, mlstm_b_f, mlstm_norm_g, diff_lq1, diff_lk1, diff_lq2, diff_lk2, diff_norm_g, w_branch, w_out, mlp_w1, mlp_w2):
    B, S, D = x_prompt.shape
    Bs, T, _ = x_sample.shape
    DEPTH = w_in.shape[0]
    n_meta = meta_tokens.shape[0]
    BLK = N_META_BLOCK
    pad = BLK - n_meta
    L = S + BLK
    Tp, Ts = B * L, Bs * T
    R = Tp + Ts
    C = conv_w.shape[2]
    CW = conv_w.shape[1]
    RANK, H, NOPE = mla_w_uk.shape[1:]
    V = mla_w_uv.shape[3]
    E = cache_mla.shape[3]
    ROPE = E - RANK
    NH, DH = mlstm_norm_g.shape[1:]
    DQK = diff_lq1.shape[1]
    DV = diff_norm_g.shape[1]
    Cb = w_branch.shape[2]
    HD = Cb // DV
    FF = mlp_w1.shape[2]
    n_pages = page_table.shape[1]
    PAGE = cache_mla.shape[2]
    past_len = n_pages * PAGE
    mla_scale = (NOPE + ROPE) ** -0.5
    df_scale = DQK ** -0.5

    GP = min(512, C)
    n_q, n_kv = H * (NOPE + ROPE), RANK + ROPE
    n_ml, n_dq, n_dk = NH * DH, HD * 2 * DQK, 2 * DQK
    off_a = 0
    off_b = off_a + 2 * C
    w_b = _round_up(n_q + n_kv, GP)
    off_c = off_b + w_b
    off_d = off_c + 4 * n_ml
    off_e = off_d + n_dq
    w_e = _round_up(n_dk + DV + 2 * NH, GP)
    off_g = off_e + w_e
    Nz = off_g + N_BRANCH * D
    assert off_b % w_b == 0 and off_c % n_ml == 0 and off_d % n_dq == 0 and off_e % n_dk == 0 and n_dk == DV
    splits = (2 * C, n_q, n_kv, 3 * n_ml, n_ml, 2 * NH, n_dq, n_dk, DV, N_BRANCH * D)
    src = [0]
    for n in splits:
        src.append(src[-1] + n)

    def cols(i):
        return w_in[:, :, src[i]:src[i + 1]].astype(jnp.bfloat16)

    zb = jnp.zeros((DEPTH, D, w_b - n_q - n_kv), jnp.bfloat16)
    ze = jnp.zeros((DEPTH, D, w_e - n_dk - DV - 2 * NH), jnp.bfloat16)
    w_in_packed = jnp.concatenate([cols(0), cols(1), cols(2), zb, cols(3), cols(4), cols(6), cols(7), cols(8),
                                   cols(5), ze, cols(9)], axis=2)
    off_if = off_e + n_dk + DV

    bm_big = _divisor(R, 1280, SUBLANE)
    bm_merge = _divisor(math.gcd(Tp, Ts), 512, SUBLANE)
    bn_in = _divisor(Nz, 512, GP)
    assert off_g % bn_in == 0
    bn_merge = _divisor(D, 256, LANE)
    bn_out = _divisor(D, 512, LANE)
    bf = _divisor(FF, 256, LANE)
    tq = BLK
    tp = _divisor(math.gcd(Tp, Ts), BLK, SUBLANE)
    assert tp == tq and tp % T == 0 and L % ML_CHUNK == 0 and L % tq == 0

    dt = x_prompt.dtype
    x_full = jnp.concatenate([jnp.zeros((B, pad, D), dt), jnp.broadcast_to(meta_tokens.astype(dt), (B, n_meta, D)),
                              x_prompt], axis=1)
    h = jnp.concatenate([x_full.reshape(Tp, D), x_sample.reshape(Ts, D)], axis=0)
    pos_p = jnp.arange(L, dtype=jnp.int32) - pad
    valid_p = (pos_p >= 0).astype(F32)
    pos_all = jnp.concatenate([jnp.tile(pos_p, B), jnp.tile(past_len + jnp.arange(T, dtype=jnp.int32), Bs)])
    row_mask = jnp.concatenate([jnp.tile(valid_p, B), jnp.ones((Ts,), F32)]).reshape(R, 1)
    half = ROPE // 2
    freq = ROPE_THETA ** (-jnp.arange(half, dtype=F32) / half)
    ang = pos_all.astype(F32)[:, None] * freq
    cos, sin = jnp.cos(ang), jnp.sin(ang)
    rope_cs = jnp.concatenate([cos, cos], axis=1)
    rope_sn = jnp.concatenate([-sin, sin], axis=1)
    slopes = 2.0 ** (-8.0 * jnp.arange(1, HD + 1, dtype=F32) / HD)
    slopes_p = jnp.tile(jnp.repeat(slopes, tq), 2).reshape(2 * HD * tq, 1)
    slopes_s = jnp.tile(jnp.repeat(slopes, T), 2).reshape(2 * HD * T, 1)
    pt_flat = page_table.reshape(-1).astype(jnp.int32)
    cache_mla_t = jnp.swapaxes(cache_mla, 2, 3)
    hist_p = jnp.zeros((B, CONV_HIST_ROWS, C), F32)
    zeros_C = jnp.zeros((1, B, NH, DH, DH), F32)
    zeros_n = jnp.zeros((1, B, NH, DH), F32)
    zeros_m = jnp.zeros((1, B, NH), F32)

    def gate_views(zif, valid, n_chunks, Lc):
        gc = jnp.concatenate([zif, valid[:, None]], axis=1)
        gr = gc.reshape(n_chunks, Lc, 2 * NH + 1).transpose(0, 2, 1)
        return gc, gr

    outs = [[] for _ in range(14)]
    for l in range(DEPTH):
        lam_init = 0.8 - 0.6 * math.exp(-0.3 * l)
        z = _inproj(h, norm_mix_g[l], w_in_packed, l, bm_big, bn_in)

        a_p, u_p = _conv(z, 0, B, L, tq, hist_p, conv_w[l], conv_b[l], conv_ln_g[l], conv_ln_b[l], C)
        hist_s = jnp.pad(state_conv[l], ((0, 0), (CONV_HIST_ROWS - (CW - 1), 0), (0, 0)))
        a_s, u_s = _conv(z, Tp, Bs, T, T, hist_s, conv_w[l], conv_b[l], conv_ln_g[l], conv_ln_b[l], C)

        wuk_t = jnp.transpose(mla_w_uk[l], (1, 2, 0))
        wuv_t = jnp.transpose(mla_w_uv[l], (1, 0, 2))
        qabs, rows = _mla_prep(z, off_b // w_b, w_b, tp, rope_cs, rope_sn, wuk_t, mla_kv_g[l], n_q)
        b_p = _mla_prompt(qabs, rows, wuv_t, B, L, tq, mla_scale, pad)
        qabs_s = qabs[Tp // tp:].reshape(Ts // tp, H, tp // T, T, E)
        b_s = _mla_sample(qabs_s, rows, Tp, cache_mla_t, l, pt_flat, wuv_t, Bs, T, n_pages, mla_scale)

        zif = z[:, off_if:off_if + 2 * NH]
        gc_p, gr_p = gate_views(zif[:Tp], jnp.tile(valid_p, B), Tp // ML_CHUNK, ML_CHUNK)
        gc_s, gr_s = gate_views(zif[Tp:], jnp.ones((Ts,), F32), Bs, T)
        c_p, C_p, n_p, m_p = _mlstm(z, 0, B, L, ML_CHUNK, off_c // n_ml, gc_p, gr_p, mlstm_b_i[l], mlstm_b_f[l],
                                    mlstm_norm_g[l], zeros_C, zeros_n, zeros_m, 0)
        c_s, C_s, n_s, m_s = _mlstm(z, Tp, Bs, T, T, off_c // n_ml, gc_s, gr_s, mlstm_b_i[l], mlstm_b_f[l],
                                    mlstm_norm_g[l], state_mlstm_C, state_mlstm_n, state_mlstm_m, l)

        lam_vecs = [v[l].reshape(1, DQK) for v in (diff_lq1, diff_lk1, diff_lq2, diff_lk2)]
        d_p = _diff_prompt(z, off_d // n_dq, off_e // n_dk, off_e // n_dk + 1, slopes_p, lam_vecs, diff_norm_g[l],
                           B, L, tq, HD, DQK, DV, df_scale, pad, lam_init)
        d_s = _diff_sample(z, Tp, off_d // n_dq, off_e // n_dk, off_e // n_dk + 1, cache_dk, cache_dv, l, pt_flat,
                           slopes_s, lam_vecs, diff_norm_g[l], Bs, T, HD, DQK, DV, n_pages, df_scale, lam_init)

        mix = _merge(z, off_g, (a_p, b_p, c_p, d_p), (a_s, b_s, c_s, d_s), w_branch, l, bm_merge, bn_merge)
        h = _mm_res(mix, w_out, l, h, bm_big, bn_out)
        h = _mlp(h, norm_mlp_g[l], mlp_w1, mlp_w2, l, row_mask, bm_big, bf)

        dk_all = z[:, off_e:off_e + n_dk]
        dv_all = z[:, off_e + n_dk:off_e + n_dk + DV]
        layer_out = (
            rows[:Tp].reshape(B, L, E)[:, pad:], rows[Tp:].reshape(Bs, T, E),
            dk_all[:Tp].reshape(B, L, n_dk)[:, pad:], dk_all[Tp:].reshape(Bs, T, n_dk),
            dv_all[:Tp].reshape(B, L, DV)[:, pad:], dv_all[Tp:].reshape(Bs, T, DV),
            u_p.reshape(B, L, C)[:, L - (CW - 1):],
            jnp.concatenate([state_conv[l][:, T:], u_s.reshape(Bs, T, C)], axis=1),
            C_p, C_s, n_p, n_s, m_p.reshape(B, NH), m_s.reshape(Bs, NH))
        for o, v in zip(outs, layer_out):
            o.append(v)

    y_prompt = _final_norm(h, norm_final_g, 0, B, L // BLK, S // BLK, BLK).reshape(B, S, D)
    bm_s = _divisor(Ts, 256, SUBLANE)
    y_sample = _final_norm(h, norm_final_g, Tp // bm_s, 1, Ts // bm_s, Ts // bm_s, bm_s).reshape(Bs, T, D)
    return (y_prompt, y_sample) + tuple(jnp.stack(o, axis=0) for o in outs)
```

```python
import functools
import math

import jax
import jax.numpy as jnp
from jax import lax
from jax.experimental import pallas as pl
from jax.experimental.pallas import tpu as pltpu

F32 = jnp.float32

N_META_BLOCK = 128
EPS = 1e-6
NEG = -1e30
ROPE_THETA = 10000.0
ML_CHUNK = 128
N_BRANCH = 4

LANE = 128
SUBLANE = 8
VMEM_PHYSICAL_BYTES = 64 * 1024 * 1024
CONV_HIST_ROWS = 32
PAGES_PER_STEP = 64


def _cparams(sem, vmem_mib):
    return pltpu.CompilerParams(dimension_semantics=sem,
                                vmem_limit_bytes=min(int(vmem_mib * 1024 * 1024), VMEM_PHYSICAL_BYTES - (4 << 20)))


def _divisor(n, pref, mult):
    if n <= pref:
        return n
    d = (pref // mult) * mult
    while d >= mult:
        if n % d == 0:
            return d
        d -= mult
    raise ValueError(f"no block for {n} (pref {pref}, multiple of {mult})")


def _rms_rows(x, g):
    ms = jnp.mean(x * x, axis=-1, keepdims=True)
    return x * lax.rsqrt(ms + EPS) * g


def _norm_into(x_ref, g_ref, xn_ref, chunk):
    g = g_ref[...]

    def body(c, carry):
        r = pl.multiple_of(c * chunk, chunk)
        xn_ref[pl.ds(r, chunk), :] = _rms_rows(x_ref[pl.ds(r, chunk), :], g).astype(xn_ref.dtype)
        return carry

    lax.fori_loop(0, x_ref.shape[0] // chunk, body, 0)


def _inproj_body(x_ref, g_ref, w_ref, o_ref, xn_ref, *, chunk):
    @pl.when(pl.program_id(1) == 0)
    def _():
        _norm_into(x_ref, g_ref, xn_ref, chunk)

    o_ref[...] = jnp.dot(xn_ref[...], w_ref[...], preferred_element_type=F32)


def _inproj(h, g, w_all, layer, bm, bn):
    R, D = h.shape
    N = w_all.shape[2]
    chunk = _divisor(bm, 64, SUBLANE)
    vmem = (2 * bm * D * 4 + bm * D * 2 + 2 * D * bn * 2 + 2 * bm * bn * 4) / 2**20 + 8
    return pl.pallas_call(
        functools.partial(_inproj_body, chunk=chunk),
        out_shape=jax.ShapeDtypeStruct((R, N), F32),
        grid=(R // bm, N // bn),
        in_specs=[pl.BlockSpec((bm, D), lambda i, j: (i, 0)),
                  pl.BlockSpec((1, D), lambda i, j: (0, 0)),
                  pl.BlockSpec((None, D, bn), lambda i, j: (layer, 0, j))],
        out_specs=pl.BlockSpec((bm, bn), lambda i, j: (i, j)),
        scratch_shapes=[pltpu.VMEM((bm, D), w_all.dtype)],
        compiler_params=_cparams(("parallel", "arbitrary"), vmem),
        name="inproj",
    )(h, g.reshape(1, D), w_all)


def _merge_body(*refs, n_prompt_blocks):
    gp = refs[0:N_BRANCH]
    br_p = refs[N_BRANCH:2 * N_BRANCH]
    br_s = refs[2 * N_BRANCH:3 * N_BRANCH]
    wb_ref, o_ref = refs[3 * N_BRANCH], refs[3 * N_BRANCH + 1]

    def mix(br):
        acc = None
        for k in range(N_BRANCH):
            p = jnp.dot(br[k][...], wb_ref[k], preferred_element_type=F32)
            t = jax.nn.sigmoid(gp[k][...]) * p
            acc = t if acc is None else acc + t
        o_ref[...] = acc

    is_prompt = pl.program_id(0) < n_prompt_blocks

    @pl.when(is_prompt)
    def _():
        mix(br_p)

    @pl.when(jnp.logical_not(is_prompt))
    def _():
        mix(br_s)


def _merge(z, gate_off, br_p, br_s, wb_all, layer, bm, bn):
    R = z.shape[0]
    _, _, Cb, D = wb_all.shape
    n_p = br_p[0].shape[0] // bm
    gspecs = [pl.BlockSpec((bm, bn), functools.partial(lambda i, j, o: (i, o + j), o=(gate_off + k * D) // bn))
              for k in range(N_BRANCH)]
    pspecs = [pl.BlockSpec((bm, Cb), lambda i, j: (jnp.minimum(i, n_p - 1), 0)) for _ in range(N_BRANCH)]
    sspecs = [pl.BlockSpec((bm, Cb), lambda i, j: (jnp.maximum(i - n_p, 0), 0)) for _ in range(N_BRANCH)]
    vmem = (2 * N_BRANCH * bm * bn + 4 * N_BRANCH * bm * Cb + 2 * N_BRANCH * Cb * bn + 4 * bm * bn) * 4 / 2**20 + 6
    return pl.pallas_call(
        functools.partial(_merge_body, n_prompt_blocks=n_p),
        out_shape=jax.ShapeDtypeStruct((R, D), F32),
        grid=(R // bm, D // bn),
        in_specs=gspecs + pspecs + sspecs + [pl.BlockSpec((None, N_BRANCH, Cb, bn), lambda i, j: (layer, 0, 0, j))],
        out_specs=pl.BlockSpec((bm, bn), lambda i, j: (i, j)),
        compiler_params=_cparams(("parallel", "arbitrary"), vmem),
        name="merge",
    )(*([z] * N_BRANCH), *br_p, *br_s, wb_all)


def _mm_res_body(x_ref, w_ref, r_ref, o_ref):
    o_ref[...] = r_ref[...] + jnp.dot(x_ref[...], w_ref[...], preferred_element_type=F32)


def _mm_res(x, w_all, layer, res, bm, bn):
    R, K = x.shape
    N = w_all.shape[2]
    vmem = (2 * bm * K + 2 * K * bn + 4 * bm * bn) * 4 / 2**20 + 6
    return pl.pallas_call(
        _mm_res_body,
        out_shape=jax.ShapeDtypeStruct((R, N), F32),
        grid=(R // bm, N // bn),
        in_specs=[pl.BlockSpec((bm, K), lambda i, j: (i, 0)),
                  pl.BlockSpec((None, K, bn), lambda i, j: (layer, 0, j)),
                  pl.BlockSpec((bm, bn), lambda i, j: (i, j))],
        out_specs=pl.BlockSpec((bm, bn), lambda i, j: (i, j)),
        compiler_params=_cparams(("parallel", "arbitrary"), vmem),
        name="outproj",
    )(x, w_all, res)


def _mlp_body(x_ref, g_ref, w1_ref, w2_ref, mk_ref, o_ref, xn_ref, hid_ref, *, chunk, nf):
    f = pl.program_id(1)
    n_chunks = x_ref.shape[0] // chunk
    bf16 = xn_ref.dtype

    def up(slot):
        hid = jnp.maximum(jnp.dot(xn_ref[...], w1_ref[...].astype(bf16), preferred_element_type=F32), 0.0)
        hid_ref[slot] = (hid * hid).astype(bf16)

    def down(slot):
        o_ref[...] += jnp.dot(hid_ref[slot], w2_ref[...].astype(bf16), preferred_element_type=F32)

    @pl.when(f == 0)
    def _():
        g = g_ref[...]

        def body(c, carry):
            r = pl.multiple_of(c * chunk, chunk)
            x = x_ref[pl.ds(r, chunk), :]
            xn_ref[pl.ds(r, chunk), :] = _rms_rows(x, g).astype(bf16)
            o_ref[pl.ds(r, chunk), :] = x
            return carry

        lax.fori_loop(0, n_chunks, body, 0)
        up(0)

    middle = (f > 0) & (f < nf)

    @pl.when(middle & (f % 2 == 1))
    def _():
        down(0)
        up(1)

    @pl.when(middle & (f % 2 == 0))
    def _():
        down(1)
        up(0)

    @pl.when(f == nf)
    def _():
        down((nf - 1) % 2)

        def body(c, carry):
            r = pl.multiple_of(c * chunk, chunk)
            o_ref[pl.ds(r, chunk), :] = o_ref[pl.ds(r, chunk), :] * mk_ref[pl.ds(r, chunk), :]
            return carry

        lax.fori_loop(0, n_chunks, body, 0)


def _mlp(h, g, w1_all, w2_all, layer, row_mask, bm, bf):
    R, D = h.shape
    FF = w1_all.shape[2]
    nf = FF // bf
    chunk = _divisor(bm, 64, SUBLANE)
    vmem = (4 * bm * D * 4 + bm * D * 2 + 2 * bm * bf * 2 + 4 * D * bf * 4 + 2 * bm * LANE * 4) / 2**20 + 6
    return pl.pallas_call(
        functools.partial(_mlp_body, chunk=chunk, nf=nf),
        out_shape=jax.ShapeDtypeStruct((R, D), F32),
        grid=(R // bm, nf + 1),
        in_specs=[pl.BlockSpec((bm, D), lambda i, f: (i, 0)),
                  pl.BlockSpec((1, D), lambda i, f: (0, 0)),
                  pl.BlockSpec((None, D, bf), lambda i, f: (layer, 0, jnp.minimum(f, nf - 1))),
                  pl.BlockSpec((None, bf, D), lambda i, f: (layer, jnp.maximum(f - 1, 0), 0)),
                  pl.BlockSpec((bm, 1), lambda i, f: (i, 0))],
        out_specs=pl.BlockSpec((bm, D), lambda i, f: (i, 0)),
        scratch_shapes=[pltpu.VMEM((bm, D), jnp.bfloat16), pltpu.VMEM((2, bm, bf), jnp.bfloat16)],
        compiler_params=_cparams(("parallel", "arbitrary"), vmem),
        name="mlp",
    )(h, g.reshape(1, D), w1_all, w2_all, row_mask)


def _final_norm_body(x_ref, g_ref, o_ref):
    o_ref[...] = _rms_rows(x_ref[...], g_ref[...])


def _final_norm(h, g, blk0, n_seq, seq_blocks, out_blocks, bm):
    D = h.shape[1]
    skip = seq_blocks - out_blocks
    return pl.pallas_call(
        _final_norm_body,
        out_shape=jax.ShapeDtypeStruct((n_seq * out_blocks * bm, D), F32),
        grid=(n_seq, out_blocks),
        in_specs=[pl.BlockSpec((bm, D), lambda s, i: (blk0 + s * seq_blocks + skip + i, 0)),
                  pl.BlockSpec((1, D), lambda s, i: (0, 0))],
        out_specs=pl.BlockSpec((bm, D), lambda s, i: (s * out_blocks + i, 0)),
        compiler_params=_cparams(("parallel", "arbitrary"), 16),
        name="final_norm",
    )(h, g.reshape(1, D))


def _conv_body(*refs, tb, C, CW, has_prev):
    if has_prev:
        ag_ref, prev_ref, hist_ref, cw_ref, cb_ref, lg_ref, lb_ref, a_ref, u_ref, ubuf, ybuf = refs
    else:
        ag_ref, hist_ref, cw_ref, cb_ref, lg_ref, lb_ref, a_ref, u_ref, ubuf, ybuf = refs
    HP = CONV_HIST_ROWS
    ag = ag_ref[...]
    u = ag[:, :C] * jax.nn.sigmoid(ag[:, C:])
    u_ref[...] = u
    ubuf[HP:, :] = u
    if has_prev:
        first = pl.program_id(1) == 0

        @pl.when(first)
        def _():
            ubuf[:HP, :] = hist_ref[0]

        @pl.when(jnp.logical_not(first))
        def _():
            p = prev_ref[...]
            ubuf[:HP, :] = p[:, :C] * jax.nn.sigmoid(p[:, C:])
    else:
        ubuf[:HP, :] = hist_ref[0]
    off = HP - (CW - 1)

    def cbody(cc, carry):
        c0 = pl.multiple_of(cc * LANE, LANE)
        acc = jnp.zeros((tb, LANE), F32)
        for w in range(CW):
            acc = acc + ubuf[pl.ds(off + w, tb), pl.ds(c0, LANE)] * cw_ref[pl.ds(w, 1), pl.ds(c0, LANE)]
        ybuf[:, pl.ds(c0, LANE)] = acc + cb_ref[:, pl.ds(c0, LANE)]
        return carry

    lax.fori_loop(0, C // LANE, cbody, 0)
    y = ybuf[...]
    yc = y - jnp.mean(y, axis=-1, keepdims=True)
    yn = yc * lax.rsqrt(jnp.mean(yc * yc, axis=-1, keepdims=True) + EPS) * lg_ref[...] + lb_ref[...]
    a_ref[...] = yn * jax.nn.sigmoid(yn)


def _conv(z, row0, nb, L, tb, hist, cw, cb, lg, lb, C):
    CW = cw.shape[0]
    HP = CONV_HIST_ROWS
    nt = L // tb
    has_prev = nt > 1
    cw_p = jnp.zeros((HP, C), F32).at[:CW].set(cw)
    rb0 = row0 // tb
    in_specs = [pl.BlockSpec((tb, 2 * C), lambda b, i: (rb0 + b * nt + i, 0))]
    args = [z]
    if has_prev:
        pb0 = row0 // HP
        in_specs.append(pl.BlockSpec((HP, 2 * C), lambda b, i: (jnp.maximum(pb0 + (b * L + i * tb) // HP - 1, 0), 0)))
        args.append(z)
    in_specs += [pl.BlockSpec((1, HP, C), lambda b, i: (b, 0, 0)),
                 pl.BlockSpec((HP, C), lambda b, i: (0, 0)),
                 pl.BlockSpec((1, C), lambda b, i: (0, 0)),
                 pl.BlockSpec((1, C), lambda b, i: (0, 0)),
                 pl.BlockSpec((1, C), lambda b, i: (0, 0))]
    args += [hist, cw_p, cb.reshape(1, C), lg.reshape(1, C), lb.reshape(1, C)]
    return pl.pallas_call(
        functools.partial(_conv_body, tb=tb, C=C, CW=CW, has_prev=has_prev),
        out_shape=(jax.ShapeDtypeStruct((nb * L, C), F32), jax.ShapeDtypeStruct((nb * L, C), F32)),
        grid=(nb, nt),
        in_specs=in_specs,
        out_specs=(pl.BlockSpec((tb, C), lambda b, i: (b * nt + i, 0)),
                   pl.BlockSpec((tb, C), lambda b, i: (b * nt + i, 0))),
        scratch_shapes=[pltpu.VMEM((HP + tb, C), F32), pltpu.VMEM((tb, C), F32)],
        compiler_params=_cparams(("parallel", "arbitrary"), 32),
        name="conv",
    )(*args)


def _mla_prep_body(zq_ref, cs_ref, sn_ref, wuk_ref, kvg_ref, qabs_ref, rows_ref, *, H, NOPE, ROPE, RANK, kv_off):
    cs = cs_ref[...]
    sn = sn_ref[...]
    half = ROPE // 2

    def rope(x):
        xs = jnp.concatenate([x[:, half:], x[:, :half]], axis=1)
        return x * cs + xs * sn

    for h in range(H):
        base = h * (NOPE + ROPE)
        qabs_ref[0, h, :, :RANK] = jnp.dot(zq_ref[:, base:base + NOPE], wuk_ref[h], preferred_element_type=F32)
        qabs_ref[0, h, :, RANK:] = rope(zq_ref[:, base + NOPE:base + NOPE + ROPE])
    rows_ref[:, :RANK] = _rms_rows(zq_ref[:, kv_off:kv_off + RANK], kvg_ref[...])
    rows_ref[:, RANK:] = rope(zq_ref[:, kv_off + RANK:kv_off + RANK + ROPE])


def _mla_prep(z, col_blk, wblk, tp, cs, sn, wuk_t, kvg, kv_off):
    R = z.shape[0]
    H, NOPE, RANK = wuk_t.shape
    ROPE = cs.shape[1]
    E = RANK + ROPE
    return pl.pallas_call(
        functools.partial(_mla_prep_body, H=H, NOPE=NOPE, ROPE=ROPE, RANK=RANK, kv_off=kv_off),
        out_shape=(jax.ShapeDtypeStruct((R // tp, H, tp, E), F32), jax.ShapeDtypeStruct((R, E), F32)),
        grid=(R // tp,),
        in_specs=[pl.BlockSpec((tp, wblk), lambda i: (i, col_blk)),
                  pl.BlockSpec((tp, ROPE), lambda i: (i, 0)),
                  pl.BlockSpec((tp, ROPE), lambda i: (i, 0)),
                  pl.BlockSpec((H, NOPE, RANK), lambda i: (0, 0, 0)),
                  pl.BlockSpec((1, RANK), lambda i: (0, 0))],
        out_specs=(pl.BlockSpec((1, H, tp, E), lambda i: (i, 0, 0, 0)),
                   pl.BlockSpec((tp, E), lambda i: (i, 0))),
        compiler_params=_cparams(("parallel",), 32),
        name="mla_prep",
    )(z, cs, sn, wuk_t, kvg.reshape(1, RANK))


def _dot_t(a, b):
    return lax.dot_general(a, b, (((1,), (1,)), ((), ())), preferred_element_type=F32)


def _softmax_step(s, v, m_ref, l_ref, acc_ref, trans_v=False):
    m_prev = m_ref[...]
    m_new = jnp.maximum(m_prev, jnp.max(s, axis=1, keepdims=True))
    alpha = jnp.exp(m_prev - m_new)
    p = jnp.exp(s - m_new)
    l_ref[...] = alpha * l_ref[...] + jnp.sum(p, axis=1, keepdims=True)
    pv = _dot_t(p, v) if trans_v else jnp.dot(p, v, preferred_element_type=F32)
    acc_ref[...] = alpha * acc_ref[...] + pv
    m_ref[...] = m_new


def _lanes(x, n):
    return x if n == 1 else jnp.concatenate([x] * n, axis=1)


def _softmax_step_wide(s, v, m_ref, acc_ref):
    w = s.shape[1]
    m_prev = m_ref[...]
    m_new = jnp.maximum(m_prev, jnp.max(s, axis=1, keepdims=True))
    alpha = jnp.exp(m_prev - m_new)
    p = jnp.exp(s - _lanes(m_new, w // LANE))
    v_ext = jnp.concatenate([v, jnp.ones((w, LANE), F32)], axis=1)
    acc_ref[...] = (_lanes(alpha, acc_ref.shape[1] // LANE) * acc_ref[...]
                    + jnp.dot(p, v_ext, preferred_element_type=F32))
    m_ref[...] = m_new


def _mla_prompt_body(q_ref, kv_ref, wuv_ref, o_ref, m_ref, acc_ref, *, H, tq, RANK, V, scale, n_pad):
    qi = pl.program_id(1)
    q = q_ref[0].reshape(H * tq, q_ref.shape[-1])
    m_ref[...] = jnp.full_like(m_ref, NEG)
    acc_ref[...] = jnp.zeros_like(acc_ref)
    t_col = qi * tq + lax.broadcasted_iota(jnp.int32, (H * tq, 1), 0) % tq

    def step(k0, w):
        k = kv_ref[pl.ds(k0, w), :]
        k_idx = k0 + lax.broadcasted_iota(jnp.int32, (1, w), 1)
        s = jnp.where((k_idx <= t_col) & (k_idx >= n_pad), _dot_t(q, k) * scale, NEG)
        _softmax_step_wide(s, k[:, :RANK], m_ref, acc_ref)

    nb = (qi + 1) // 2

    def body(j, carry):
        step(pl.multiple_of(j * 2 * tq, 2 * tq), 2 * tq)
        return carry

    lax.fori_loop(0, nb, body, 0)

    @pl.when((qi + 1) % 2 == 1)
    def _():
        step(pl.multiple_of(nb * 2 * tq, tq), tq)

    acc = acc_ref[...]
    o = acc[:, :RANK] / acc[:, RANK:RANK + 1]
    for h in range(H):
        o_ref[:, h * V:(h + 1) * V] = jnp.dot(o[h * tq:(h + 1) * tq], wuv_ref[h], preferred_element_type=F32)


def _mla_prompt(qabs, rows, wuv_t, B, L, tq, scale, n_pad):
    H, RANK, V = wuv_t.shape
    E = rows.shape[1]
    nq = L // tq
    return pl.pallas_call(
        functools.partial(_mla_prompt_body, H=H, tq=tq, RANK=RANK, V=V, scale=scale, n_pad=n_pad),
        out_shape=jax.ShapeDtypeStruct((B * L, H * V), F32),
        grid=(B, nq),
        in_specs=[pl.BlockSpec((1, H, tq, E), lambda b, i: (b * nq + i, 0, 0, 0)),
                  pl.BlockSpec((L, E), lambda b, i: (b, 0)),
                  pl.BlockSpec((H, RANK, V), lambda b, i: (0, 0, 0))],
        out_specs=pl.BlockSpec((tq, H * V), lambda b, i: (b * nq + i, 0)),
        scratch_shapes=[pltpu.VMEM((H * tq, LANE), F32), pltpu.VMEM((H * tq, RANK + LANE), F32)],
        compiler_params=_cparams(("parallel", "arbitrary"), 40),
        name="mla_prompt",
    )(qabs, rows, wuv_t)


def _mla_sample_body(pt_ref, q_ref, new_ref, *refs, H, T, RANK, V, scale, PP):
    pages = refs[:PP]
    wuv_ref, o_ref, m_ref, l_ref, acc_ref = refs[PP:]
    g = pl.program_id(1)
    q = q_ref[0, :, 0].reshape(H * T, q_ref.shape[-1])

    @pl.when(g == 0)
    def _():
        kn = new_ref[...]
        t_idx = lax.broadcasted_iota(jnp.int32, (H * T, T), 0) % T
        s_idx = lax.broadcasted_iota(jnp.int32, (H * T, T), 1)
        s = jnp.where(s_idx <= t_idx, _dot_t(q, kn) * scale, NEG)
        m0 = jnp.max(s, axis=1, keepdims=True)
        p = jnp.exp(s - m0)
        m_ref[...] = m0
        l_ref[...] = jnp.sum(p, axis=1, keepdims=True)
        acc_ref[...] = jnp.dot(p, kn[:, :RANK], preferred_element_type=F32)

    kt = jnp.concatenate([pg[0, 0] for pg in pages], axis=1)
    _softmax_step(jnp.dot(q, kt, preferred_element_type=F32) * scale, kt[:RANK], m_ref, l_ref, acc_ref, trans_v=True)

    @pl.when(g == pl.num_programs(1) - 1)
    def _():
        o = acc_ref[...] / l_ref[...]
        for h in range(H):
            o_ref[:, h * V:(h + 1) * V] = jnp.dot(o[h * T:(h + 1) * T], wuv_ref[h], preferred_element_type=F32)


def _mla_sample(qabs_s, rows, row0, cache_t, layer, pt_flat, wuv_t, Bs, T, n_pages, scale):
    H, RANK, V = wuv_t.shape
    E = rows.shape[1]
    per_blk = qabs_s.shape[2]
    PAGE = cache_t.shape[3]
    PP = min(PAGES_PER_STEP, n_pages)
    G = n_pages // PP
    rb0 = row0 // T
    page_specs = [pl.BlockSpec((1, 1, E, PAGE),
                               functools.partial(lambda b, g, pt, kk: (layer, pt[b * n_pages + g * PP + kk], 0, 0), kk=kk))
                  for kk in range(PP)]
    grid_spec = pltpu.PrefetchScalarGridSpec(
        num_scalar_prefetch=1,
        grid=(Bs, G),
        in_specs=[pl.BlockSpec((1, H, 1, T, E), lambda b, g, pt: (b // per_blk, 0, b % per_blk, 0, 0)),
                  pl.BlockSpec((T, E), lambda b, g, pt: (rb0 + b, 0))] + page_specs +
                 [pl.BlockSpec((H, RANK, V), lambda b, g, pt: (0, 0, 0))],
        out_specs=pl.BlockSpec((T, H * V), lambda b, g, pt: (b, 0)),
        scratch_shapes=[pltpu.VMEM((H * T, 1), F32), pltpu.VMEM((H * T, 1), F32), pltpu.VMEM((H * T, RANK), F32)],
    )
    return pl.pallas_call(
        functools.partial(_mla_sample_body, H=H, T=T, RANK=RANK, V=V, scale=scale, PP=PP),
        out_shape=jax.ShapeDtypeStruct((Bs * T, H * V), F32),
        grid_spec=grid_spec,
        compiler_params=_cparams(("parallel", "arbitrary"), 40),
        name="mla_sample",
    )(pt_flat, qabs_s, rows, *([cache_t] * PP), wuv_t)


def _diff_finish(a0, l0, a1, l1, lq1, lk1, lq2, lk2, ng, lam_init):
    lam = (jnp.exp(jnp.sum(lq1 * lk1, axis=1, keepdims=True))
           - jnp.exp(jnp.sum(lq2 * lk2, axis=1, keepdims=True)) + lam_init)
    d = a0 / l0 - lam * (a1 / l1)
    return _rms_rows(d, ng) * (1.0 - lam_init)


def _stack_maps(q_ref, H, W, DQK):
    q = jnp.concatenate([q_ref[:, h * W:(h + 1) * W] for h in range(H)], axis=0)
    first_map = lax.broadcasted_iota(jnp.int32, q.shape, 1) < DQK
    return jnp.concatenate([jnp.where(first_map, q, 0.0), jnp.where(first_map, 0.0, q)], axis=0)


def _diff_out(o_ref, a, l, lam_refs, ng_ref, H, n, DV, lam_init):
    HT = H * n
    d = _diff_finish(a[:HT], l[:HT], a[HT:], l[HT:], *[r[...] for r in lam_refs], ng_ref[...], lam_init)
    for h in range(H):
        o_ref[:, h * DV:(h + 1) * DV] = d[h * n:(h + 1) * n]


def _diff_prompt_body(q_ref, k_ref, v_ref, slope_ref, lq1_ref, lk1_ref, lq2_ref, lk2_ref, ng_ref, o_ref,
                      m_ref, a_ref, *, H, tq, DQK, DV, scale, n_pad, lam_init):
    qi = pl.program_id(1)
    qs = _stack_maps(q_ref, H, 2 * DQK, DQK)
    m_ref[...] = jnp.full_like(m_ref, NEG)
    a_ref[...] = jnp.zeros_like(a_ref)
    t_col = (qi * tq + lax.broadcasted_iota(jnp.int32, (2 * H * tq, 1), 0) % tq).astype(F32)
    slope = slope_ref[...]

    def step(k0, w):
        k = k_ref[pl.ds(k0, w), :]
        v = v_ref[pl.ds(k0, w), :]
        k_idx = k0 + lax.broadcasted_iota(jnp.int32, (1, w), 1)
        dist = t_col - k_idx.astype(F32)
        s = jnp.where((dist >= 0.0) & (k_idx >= n_pad), _dot_t(qs, k) * scale - slope * dist, NEG)
        _softmax_step_wide(s, v, m_ref, a_ref)

    nb = (qi + 1) // 2

    def body(j, carry):
        step(pl.multiple_of(j * 2 * tq, 2 * tq), 2 * tq)
        return carry

    lax.fori_loop(0, nb, body, 0)

    @pl.when((qi + 1) % 2 == 1)
    def _():
        step(pl.multiple_of(nb * 2 * tq, tq), tq)

    acc = a_ref[...]
    _diff_out(o_ref, acc[:, :DV], acc[:, DV:DV + 1], (lq1_ref, lk1_ref, lq2_ref, lk2_ref), ng_ref, H, tq, DV,
              lam_init)


def _diff_prompt(z, q_blk, k_blk, v_blk, slopes_col, lam_vecs, ng, B, L, tq, H, DQK, DV, scale, n_pad, lam_init):
    nq = L // tq
    W = 2 * DQK
    M = 2 * H * tq

    def cmap(b, i):
        return (0, 0)

    small = [pl.BlockSpec((1, DQK), cmap)] * 4 + [pl.BlockSpec((1, DV), cmap)]
    return pl.pallas_call(
        functools.partial(_diff_prompt_body, H=H, tq=tq, DQK=DQK, DV=DV, scale=scale, n_pad=n_pad, lam_init=lam_init),
        out_shape=jax.ShapeDtypeStruct((B * L, H * DV), F32),
        grid=(B, nq),
        in_specs=[pl.BlockSpec((tq, H * W), lambda b, i: (b * nq + i, q_blk)),
                  pl.BlockSpec((L, W), lambda b, i: (b, k_blk)),
                  pl.BlockSpec((L, DV), lambda b, i: (b, v_blk)),
                  pl.BlockSpec((M, 1), cmap)] + small,
        out_specs=pl.BlockSpec((tq, H * DV), lambda b, i: (b * nq + i, 0)),
        scratch_shapes=[pltpu.VMEM((M, LANE), F32), pltpu.VMEM((M, DV + LANE), F32)],
        compiler_params=_cparams(("parallel", "arbitrary"), 48),
        name="diff_prompt",
    )(z, z, z, slopes_col, *lam_vecs, ng.reshape(1, DV))


def _diff_sample_body(pt_ref, q_ref, kn_ref, vn_ref, *refs, H, T, DQK, DV, PAGE, past_len, scale, PP, lam_init):
    kpages = refs[:PP]
    vpages = refs[PP:2 * PP]
    slope_ref, lq1_ref, lk1_ref, lq2_ref, lk2_ref, ng_ref, o_ref, m_ref, l_ref, a_ref = refs[2 * PP:]
    g = pl.program_id(1)
    M = 2 * H * T
    qs = _stack_maps(q_ref, H, 2 * DQK, DQK)
    slope = slope_ref[...]
    t_col = lax.broadcasted_iota(jnp.int32, (M, 1), 0) % T

    @pl.when(g == 0)
    def _():
        s_idx = lax.broadcasted_iota(jnp.int32, (1, T), 1)
        dist = (t_col - s_idx).astype(F32)
        s = jnp.where(dist >= 0.0, _dot_t(qs, kn_ref[...]) * scale - slope * dist, NEG)
        m0 = jnp.max(s, axis=1, keepdims=True)
        p = jnp.exp(s - m0)
        m_ref[...] = m0
        l_ref[...] = jnp.sum(p, axis=1, keepdims=True)
        a_ref[...] = jnp.dot(p, vn_ref[...], preferred_element_type=F32)

    k_all = jnp.concatenate([kp[0, 0] for kp in kpages], axis=0)
    v_all = jnp.concatenate([vp[0, 0] for vp in vpages], axis=0)
    k_pos = g * (PP * PAGE) + lax.broadcasted_iota(jnp.int32, (1, PP * PAGE), 1)
    dist = (past_len + t_col).astype(F32) - k_pos.astype(F32)
    _softmax_step(_dot_t(qs, k_all) * scale - slope * dist, v_all, m_ref, l_ref, a_ref)

    @pl.when(g == pl.num_programs(1) - 1)
    def _():
        _diff_out(o_ref, a_ref[...], l_ref[...], (lq1_ref, lk1_ref, lq2_ref, lk2_ref), ng_ref, H, T, DV, lam_init)


def _diff_sample(z, row0, q_blk, k_blk, v_blk, cache_dk, cache_dv, layer, pt_flat, slopes_col, lam_vecs, ng,
                 Bs, T, H, DQK, DV, n_pages, scale, lam_init):
    W = 2 * DQK
    M = 2 * H * T
    PAGE = cache_dk.shape[2]
    PP = min(PAGES_PER_STEP, n_pages)
    G = n_pages // PP
    rb0 = row0 // T

    def pmap(kk):
        return functools.partial(lambda b, g, pt, kk: (layer, pt[b * n_pages + g * PP + kk], 0, 0), kk=kk)

    kspecs = [pl.BlockSpec((1, 1, PAGE, W), pmap(kk)) for kk in range(PP)]
    vspecs = [pl.BlockSpec((1, 1, PAGE, DV), pmap(kk)) for kk in range(PP)]

    def cmap(b, g, pt):
        return (0, 0)

    small = [pl.BlockSpec((M, 1), cmap)] + [pl.BlockSpec((1, DQK), cmap)] * 4 + [pl.BlockSpec((1, DV), cmap)]
    grid_spec = pltpu.PrefetchScalarGridSpec(
        num_scalar_prefetch=1,
        grid=(Bs, G),
        in_specs=[pl.BlockSpec((T, H * W), lambda b, g, pt: (rb0 + b, q_blk)),
                  pl.BlockSpec((T, W), lambda b, g, pt: (rb0 + b, k_blk)),
                  pl.BlockSpec((T, DV), lambda b, g, pt: (rb0 + b, v_blk))] + kspecs + vspecs + small,
        out_specs=pl.BlockSpec((T, H * DV), lambda b, g, pt: (b, 0)),
        scratch_shapes=[pltpu.VMEM((M, 1), F32), pltpu.VMEM((M, 1), F32), pltpu.VMEM((M, DV), F32)],
    )
    return pl.pallas_call(
        functools.partial(_diff_sample_body, H=H, T=T, DQK=DQK, DV=DV, PAGE=PAGE, past_len=n_pages * PAGE,
                          scale=scale, PP=PP, lam_init=lam_init),
        out_shape=jax.ShapeDtypeStruct((Bs * T, H * DV), F32),
        grid_spec=grid_spec,
        compiler_params=_cparams(("parallel", "arbitrary"), 40),
        name="diff_sample",
    )(pt_flat, z, z, z, *([cache_dk] * PP), *([cache_dv] * PP), slopes_col, *lam_vecs, ng.reshape(1, DV))


def _log_sigmoid(x):
    return jnp.minimum(x, 0.0) - jnp.log1p(jnp.exp(-jnp.abs(x)))


def _mlstm_body(q_ref, k_ref, v_ref, og_ref, gc_ref, gr_ref, bi_ref, bf_ref, ng_ref, c0_ref, n0_ref, m0_ref,
                h_ref, c_ref, n_ref, m_ref, *, NH, DH, Lc):
    @pl.when(pl.program_id(1) == 0)
    def _():
        c_ref[...] = c0_ref[...]
        n_ref[...] = n0_ref[...]
        m_ref[...] = m0_ref[...]

    row = lax.broadcasted_iota(jnp.int32, (Lc, Lc), 0)
    col = lax.broadcasted_iota(jnp.int32, (Lc, Lc), 1)
    tril = col <= row
    valid_c = gc_ref[:, 2 * NH:2 * NH + 1] > 0.0
    valid_r = gr_ref[0, 2 * NH:2 * NH + 1, :] > 0.0
    for hh in range(NH):
        sl = slice(hh * DH, (hh + 1) * DH)
        q = q_ref[:, sl]
        k = k_ref[:, sl] * (DH ** -0.5)
        v = v_ref[:, sl]
        b_i = bi_ref[:, hh:hh + 1]
        b_f = bf_ref[:, hh:hh + 1]
        i_c = jnp.where(valid_c, gc_ref[:, hh:hh + 1] + b_i, NEG)
        i_r = jnp.where(valid_r, gr_ref[0, hh:hh + 1, :] + b_i, NEG)
        lf_c = jnp.where(valid_c, _log_sigmoid(gc_ref[:, NH + hh:NH + hh + 1] + b_f), 0.0)
        lf_r = jnp.where(valid_r, _log_sigmoid(gr_ref[0, NH + hh:NH + hh + 1, :] + b_f), 0.0)
        b_c = jnp.sum(jnp.where(tril, lf_r, 0.0), axis=1, keepdims=True)
        b_r = jnp.sum(jnp.where(row <= col, lf_c, 0.0), axis=0, keepdims=True)
        C = c_ref[0, hh]
        n = n_ref[0, hh:hh + 1, :]
        m = m_ref[0, :, hh:hh + 1]
        D = jnp.where(tril, b_c - b_r + i_r, NEG)
        g_prev = b_c + m
        m_t = jnp.maximum(g_prev, jnp.max(D, axis=1, keepdims=True))
        a_prev = jnp.exp(g_prev - m_t)
        wqk = jnp.exp(D - m_t) * _dot_t(q, k)
        num = a_prev * _dot_t(q, C) + jnp.dot(wqk, v, preferred_element_type=F32)
        den = a_prev * jnp.sum(q * n, axis=1, keepdims=True) + jnp.sum(wqk, axis=1, keepdims=True)
        hv = num / jnp.maximum(jnp.abs(den), jnp.exp(-m_t))
        m_new = m_t[Lc - 1:Lc, :]
        b_last = b_c[Lc - 1:Lc, :]
        w_end = jnp.exp(b_last - b_c + i_c - m_new)
        a_end = jnp.exp(b_last + m - m_new)
        c_ref[0, hh] = a_end * C + lax.dot_general(w_end * v, k, (((0,), (0,)), ((), ())),
                                                   preferred_element_type=F32)
        n_ref[0, hh:hh + 1, :] = a_end * n + jnp.sum(w_end * k, axis=0, keepdims=True)
        m_ref[0, :, hh:hh + 1] = m_new
        hn = _rms_rows(hv, ng_ref[hh:hh + 1, :])
        h_ref[:, sl] = jax.nn.sigmoid(og_ref[:, sl]) * hn


def _mlstm(z, row0, nb, L, Lc, qkv_blk, gates_c, gates_r, b_i, b_f, ng, C0, n0, m0, layer):
    NH, DH = ng.shape
    nc = L // Lc
    rb0 = row0 // Lc
    G = gates_c.shape[1]
    Wd = NH * DH

    def zspec(cb):
        return pl.BlockSpec((Lc, Wd), functools.partial(lambda b, c, cb: (rb0 + b * nc + c, cb), cb=cb))

    st_specs = [pl.BlockSpec((1, NH, DH, DH), lambda b, c: (b, 0, 0, 0)),
                pl.BlockSpec((1, NH, DH), lambda b, c: (b, 0, 0)),
                pl.BlockSpec((1, 1, NH), lambda b, c: (b, 0, 0))]
    st_in_specs = [pl.BlockSpec((None, 1, NH, DH, DH), lambda b, c: (layer, b, 0, 0, 0)),
                   pl.BlockSpec((None, 1, NH, DH), lambda b, c: (layer, b, 0, 0)),
                   pl.BlockSpec((None, 1, 1, NH), lambda b, c: (layer, b, 0, 0))]
    return pl.pallas_call(
        functools.partial(_mlstm_body, NH=NH, DH=DH, Lc=Lc),
        out_shape=(jax.ShapeDtypeStruct((nb * L, Wd), F32),
                   jax.ShapeDtypeStruct((nb, NH, DH, DH), F32),
                   jax.ShapeDtypeStruct((nb, NH, DH), F32),
                   jax.ShapeDtypeStruct((nb, 1, NH), F32)),
        grid=(nb, nc),
        in_specs=[zspec(qkv_blk), zspec(qkv_blk + 1), zspec(qkv_blk + 2), zspec(qkv_blk + 3),
                  pl.BlockSpec((Lc, G), lambda b, c: (b * nc + c, 0)),
                  pl.BlockSpec((1, G, Lc), lambda b, c: (b * nc + c, 0, 0)),
                  pl.BlockSpec((1, NH), lambda b, c: (0, 0)),
                  pl.BlockSpec((1, NH), lambda b, c: (0, 0)),
                  pl.BlockSpec((NH, DH), lambda b, c: (0, 0))] + st_in_specs,
        out_specs=(pl.BlockSpec((Lc, Wd), lambda b, c: (b * nc + c, 0)), *st_specs),
        compiler_params=_cparams(("parallel", "arbitrary"), 32),
        name="mlstm",
    )(z, z, z, z, gates_c, gates_r, b_i.reshape(1, NH), b_f.reshape(1, NH), ng, C0, n0,
      m0.reshape(m0.shape[0], nb, 1, NH))


def _round_up(n, m):
    return (n + m - 1) // m * m


def kernel(x_prompt, x_sample, cache_mla, cache_dk, cache_dv, state_conv, state_mlstm_C, state_mlstm_n, state_mlstm_m, page_table, meta_tokens, norm_mix_g, norm_mlp_g, norm_final_g, w_in, conv_w, conv_b, conv_ln_g, conv_ln_b, mla_kv_g, mla_w_uk, mla_w_uv, mlstm_b_i, mlstm_b_f, mlstm_norm_g, diff_lq1, diff_lk1, diff_lq2, diff_lk2, diff_norm_g, w_branch, w_out, mlp_w1, mlp_w2):
    B, S, D = x_prompt.shape
    Bs, T, _ = x_sample.shape
    DEPTH = w_in.shape[0]
    n_meta = meta_tokens.shape[0]
    BLK = N_META_BLOCK
    pad = BLK - n_meta
    L = S + BLK
    Tp, Ts = B * L, Bs * T
    R = Tp + Ts
    C = conv_w.shape[2]
    CW = conv_w.shape[1]
    RANK, H, NOPE = mla_w_uk.shape[1:]
    V = mla_w_uv.shape[3]
    E = cache_mla.shape[3]
    ROPE = E - RANK
    NH, DH = mlstm_norm_g.shape[1:]
    DQK = diff_lq1.shape[1]
    DV = diff_norm_g.shape[1]
    Cb = w_branch.shape[2]
    HD = Cb // DV
    FF = mlp_w1.shape[2]
    n_pages = page_table.shape[1]
    PAGE = cache_mla.shape[2]
    past_len = n_pages * PAGE
    mla_scale = (NOPE + ROPE) ** -0.5
    df_scale = DQK ** -0.5

    GP = min(512, C)
    n_q, n_kv = H * (NOPE + ROPE), RANK + ROPE
    n_ml, n_dq, n_dk = NH * DH, HD * 2 * DQK, 2 * DQK
    off_a = 0
    off_b = off_a + 2 * C
    w_b = _round_up(n_q + n_kv, GP)
    off_c = off_b + w_b
    off_d = off_c + 4 * n_ml
    off_e = off_d + n_dq
    w_e = _round_up(n_dk + DV + 2 * NH, GP)
    off_g = off_e + w_e
    Nz = off_g + N_BRANCH * D
    assert off_b % w_b == 0 and off_c % n_ml == 0 and off_d % n_dq == 0 and off_e % n_dk == 0 and n_dk == DV
    splits = (2 * C, n_q, n_kv, 3 * n_ml, n_ml, 2 * NH, n_dq, n_dk, DV, N_BRANCH * D)
    src = [0]
    for n in splits:
        src.append(src[-1] + n)

    def cols(i):
        return w_in[:, :, src[i]:src[i + 1]].astype(jnp.bfloat16)

    zb = jnp.zeros((DEPTH, D, w_b - n_q - n_kv), jnp.bfloat16)
    ze = jnp.zeros((DEPTH, D, w_e - n_dk - DV - 2 * NH), jnp.bfloat16)
    w_in_packed = jnp.concatenate([cols(0), cols(1), cols(2), zb, cols(3), cols(4), cols(6), cols(7), cols(8),
                                   cols(5), ze, cols(9)], axis=2)
    off_if = off_e + n_dk + DV

    bm_big = _divisor(R, 1280, SUBLANE)
    bm_merge = _divisor(math.gcd(Tp, Ts), 512, SUBLANE)
    bn_in = _divisor(Nz, 512, GP)
    assert off_g % bn_in == 0
    bn_merge = _divisor(D, 256, LANE)
    bn_out = _divisor(D, 512, LANE)
    bf = _divisor(FF, 256, LANE)
    tq = BLK
    tp = _divisor(math.gcd(Tp, Ts), BLK, SUBLANE)
    assert tp == tq and tp % T == 0 and L % ML_CHUNK == 0 and L % tq == 0

    dt = x_prompt.dtype
    x_full = jnp.concatenate([jnp.zeros((B, pad, D), dt), jnp.broadcast_to(meta_tokens.astype(dt), (B, n_meta, D)),
                              x_prompt], axis=1)
    h = jnp.concatenate([x_full.reshape(Tp, D), x_sample.reshape(Ts, D)], axis=0)
    pos_p = jnp.arange(L, dtype=jnp.int32) - pad
    valid_p = (pos_p >= 0).astype(F32)
    pos_all = jnp.concatenate([jnp.tile(pos_p, B), jnp.tile(past_len + jnp.arange(T, dtype=jnp.int32), Bs)])
    row_mask = jnp.concatenate([jnp.tile(valid_p, B), jnp.ones((Ts,), F32)]).reshape(R, 1)
    half = ROPE // 2
    freq = ROPE_THETA ** (-jnp.arange(half, dtype=F32) / half)
    ang = pos_all.astype(F32)[:, None] * freq
    cos, sin = jnp.cos(ang), jnp.sin(ang)
    rope_cs = jnp.concatenate([cos, cos], axis=1)
    rope_sn = jnp.concatenate([-sin, sin], axis=1)
    slopes = 2.0 ** (-8.0 * jnp.arange(1, HD + 1, dtype=F32) / HD)
    slopes_p = jnp.tile(jnp.repeat(slopes, tq), 2).reshape(2 * HD * tq, 1)
    slopes_s = jnp.tile(jnp.repeat(slopes, T), 2).reshape(2 * HD * T, 1)
    pt_flat = page_table.reshape(-1).astype(jnp.int32)
    cache_mla_t = jnp.swapaxes(cache_mla, 2, 3)
    hist_p = jnp.zeros((B, CONV_HIST_ROWS, C), F32)
    zeros_C = jnp.zeros((1, B, NH, DH, DH), F32)
    zeros_n = jnp.zeros((1, B, NH, DH), F32)
    zeros_m = jnp.zeros((1, B, NH), F32)

    def gate_views(zif, valid, n_chunks, Lc):
        gc = jnp.concatenate([zif, valid[:, None]], axis=1)
        gr = gc.reshape(n_chunks, Lc, 2 * NH + 1).transpose(0, 2, 1)
        return gc, gr

    outs = [[] for _ in range(14)]
    for l in range(DEPTH):
        lam_init = 0.8 - 0.6 * math.exp(-0.3 * l)
        z = _inproj(h, norm_mix_g[l], w_in_packed, l, bm_big, bn_in)

        a_p, u_p = _conv(z, 0, B, L, tq, hist_p, conv_w[l], conv_b[l], conv_ln_g[l], conv_ln_b[l], C)
        hist_s = jnp.pad(state_conv[l], ((0, 0), (CONV_HIST_ROWS - (CW - 1), 0), (0, 0)))
        a_s, u_s = _conv(z, Tp, Bs, T, T, hist_s, conv_w[l], conv_b[l], conv_ln_g[l], conv_ln_b[l], C)

        wuk_t = jnp.transpose(mla_w_uk[l], (1, 2, 0))
        wuv_t = jnp.transpose(mla_w_uv[l], (1, 0, 2))
        qabs, rows = _mla_prep(z, off_b // w_b, w_b, tp, rope_cs, rope_sn, wuk_t, mla_kv_g[l], n_q)
        b_p = _mla_prompt(qabs, rows, wuv_t, B, L, tq, mla_scale, pad)
        qabs_s = qabs[Tp // tp:].reshape(Ts // tp, H, tp // T, T, E)
        b_s = _mla_sample(qabs_s, rows, Tp, cache_mla_t, l, pt_flat, wuv_t, Bs, T, n_pages, mla_scale)

        zif = z[:, off_if:off_if + 2 * NH]
        gc_p, gr_p = gate_views(zif[:Tp], jnp.tile(valid_p, B), Tp // ML_CHUNK, ML_CHUNK)
        gc_s, gr_s = gate_views(zif[Tp:], jnp.ones((Ts,), F32), Bs, T)
        c_p, C_p, n_p, m_p = _mlstm(z, 0, B, L, ML_CHUNK, off_c // n_ml, gc_p, gr_p, mlstm_b_i[l], mlstm_b_f[l],
                                    mlstm_norm_g[l], zeros_C, zeros_n, zeros_m, 0)
        c_s, C_s, n_s, m_s = _mlstm(z, Tp, Bs, T, T, off_c // n_ml, gc_s, gr_s, mlstm_b_i[l], mlstm_b_f[l],
                                    mlstm_norm_g[l], state_mlstm_C, state_mlstm_n, state_mlstm_m, l)

        lam_vecs = [v[l].reshape(1, DQK) for v in (diff_lq1, diff_lk1, diff_lq2, diff_lk2)]
        d_p = _diff_prompt(z, off_d // n_dq, off_e // n_dk, off_e // n_dk + 1, slopes_p, lam_vecs, diff_norm_g[l],
                           B, L, tq, HD, DQK, DV, df_scale, pad, lam_init)
        d_s = _diff_sample(z, Tp, off_d // n_dq, off_e // n_dk, off_e // n_dk + 1, cache_dk, cache_dv, l, pt_flat,
                           slopes_s, lam_vecs, diff_norm_g[l], Bs, T, HD, DQK, DV, n_pages, df_scale, lam_init)

        mix = _merge(z, off_g, (a_p, b_p, c_p, d_p), (a_s, b_s, c_s, d_s), w_branch, l, bm_merge, bn_merge)
        h = _mm_res(mix, w_out, l, h, bm_big, bn_out)
        h = _mlp(h, norm_mlp_g[l], mlp_w1, mlp_w2, l, row_mask, bm_big, bf)

        dk_all = z[:, off_e:off_e + n_dk]
        dv_all = z[:, off_e + n_dk:off_e + n_dk + DV]
        layer_out = (
            rows[:Tp].reshape(B, L, E)[:, pad:], rows[Tp:].reshape(Bs, T, E),
            dk_all[:Tp].reshape(B, L, n_dk)[:, pad:], dk_all[Tp:].reshape(Bs, T, n_dk),
            dv_all[:Tp].reshape(B, L, DV)[:, pad:], dv_all[Tp:].reshape(Bs, T, DV),
            u_p.reshape(B, L, C)[:, L - (CW - 1):],
            jnp.concatenate([state_conv[l][:, T:], u_s.reshape(Bs, T, C)], axis=1),
            C_p, C_s, n_p, n_s, m_p.reshape(B, NH), m_s.reshape(Bs, NH))
        for o, v in zip(outs, layer_out):
            o.append(v)

    y_prompt = _final_norm(h, norm_final_g, 0, B, L // BLK, S // BLK, BLK).reshape(B, S, D)
    bm_s = _divisor(Ts, 256, SUBLANE)
    y_sample = _final_norm(h, norm_final_g, Tp // bm_s, 1, Ts // bm_s, Ts // bm_s, bm_s).reshape(Bs, T, D)
    return (y_prompt, y_sample) + tuple(jnp.stack(o, axis=0) for o in outs)
```
